```python
import jax
import jax.numpy as jnp
from jax import lax
import numpy as np

D_MODEL = 1024
BATCH = 2
SEQ = 16384
DEPTH = 1
DEC_BATCH = 8
DEC_SEQ = 8192
PAST_LEN = 128

GRID_W = 64
QBLK = 128
HD = 64
N_HEADS_A = 8
N_KV_A = 2
GROUP_A = N_HEADS_A // N_KV_A
N_HEADS_B = 8
N_HEADS_M = 4
HD_M = 128
N_MEM = 256
WIN_R_MAX = 8
WIN_C = 16
BRANCH_W = 512
N_BRANCH = 3
ROPE_THETA = 10000.0
ROPE_FREQS = HD // 4
EPS_QK = 1e-6
EPS_LN = 1e-5
DN_ALPHA = (2 * DEPTH) ** 0.25
DN_BETA = (8 * DEPTH) ** -0.25
NEG_INF = -1e30
SPLIT_SIZES = (N_HEADS_A * HD, N_KV_A * HD, N_KV_A * HD, BRANCH_W,
               N_HEADS_B * HD, N_HEADS_B * HD, N_HEADS_B * HD, BRANCH_W,
               N_HEADS_M * HD_M, BRANCH_W, N_BRANCH * D_MODEL)
SPLIT_POINTS = tuple(sum(SPLIT_SIZES[:i + 1]) for i in range(len(SPLIT_SIZES) - 1))
D_IN = sum(SPLIT_SIZES)

kernel_name = 'hybrid_gqa_natten_mem_encoder'


def _rms_norm(x, g):
    xf = x.astype(jnp.float32)
    y = xf * lax.rsqrt(jnp.mean(xf * xf, axis=-1, keepdims=True) + EPS_QK)
    return (y * g.astype(jnp.float32)).astype(x.dtype)


def _layer_norm(x, g, b):
    xf = x.astype(jnp.float32)
    mu = jnp.mean(xf, axis=-1, keepdims=True)
    var = jnp.mean(jnp.square(xf - mu), axis=-1, keepdims=True)
    y = (xf - mu) * lax.rsqrt(var + EPS_LN) * g.astype(jnp.float32) + b.astype(jnp.float32)
    return y.astype(x.dtype)


def _axial_rope_tables(n_tok):
    t = jnp.arange(n_tok)
    pos = jnp.stack([t // GRID_W, t % GRID_W], axis=-1).astype(jnp.float32)
    inv_freq = ROPE_THETA ** (-jnp.arange(ROPE_FREQS, dtype=jnp.float32) / ROPE_FREQS)
    ang = pos[:, :, None] * inv_freq
    return jnp.cos(ang), jnp.sin(ang)


def _apply_axial_rope(x, cos, sin):
    b, s, h, d = x.shape
    xf = x.astype(jnp.float32).reshape(b, s, h, 2, 2, ROPE_FREQS)
    x1, x2 = xf[..., 0, :], xf[..., 1, :]
    c = cos[None, :, None]
    sn = sin[None, :, None]
    out = jnp.stack([x1 * c - x2 * sn, x2 * c + x1 * sn], axis=-2)
    return out.reshape(b, s, h, d).astype(x.dtype)


def _global_gqa(q, k, v):
    b, s, _, _ = q.shape
    nb = s // QBLK
    qb = q.reshape(b, nb, QBLK, N_KV_A, GROUP_A, HD).transpose(1, 0, 2, 3, 4, 5)
    scale = HD ** -0.5

    def one_block(qi):
        sc = jnp.einsum('bqkgd,bskd->bkgqs', qi, k, preferred_element_type=jnp.float32) * scale
        p = jax.nn.softmax(sc, axis=-1).astype(v.dtype)
        return jnp.einsum('bkgqs,bskd->bqkgd', p, v)

    out = lax.map(one_block, qb)
    return out.transpose(1, 0, 2, 3, 4, 5).reshape(b, s, N_HEADS_A * HD)


def _neighbourhood_attn(q, k, v, rpb):
    b, s, h, d = q.shape
    rows = s // GRID_W
    win_r = min(WIN_R_MAX, rows)
    reg_r = min(win_r + 1, rows)
    rows_per_blk = QBLK // GRID_W
    nb = s // QBLK
    n_reg = reg_r * GRID_W
    kg = k.reshape(b, rows, GRID_W, h, d)
    vg = v.reshape(b, rows, GRID_W, h, d)
    qb = q.reshape(b, nb, QBLK, h, d).transpose(1, 0, 2, 3, 4)
    q_loc = jnp.arange(QBLK)
    k_loc = jnp.arange(n_reg)
    q_col = q_loc % GRID_W
    k_col = k_loc % GRID_W
    q_wc = jnp.clip(q_col - WIN_C // 2, 0, GRID_W - WIN_C)
    col_ok = (k_col[None] >= q_wc[:, None]) & (k_col[None] < q_wc[:, None] + WIN_C)
    dc = jnp.clip(k_col[None] - q_col[:, None] + WIN_C - 1, 0, 2 * WIN_C - 2)
    scale = d ** -0.5

    def one_block(args):
        i, qi = args
        r0 = i * rows_per_blk
        rs = jnp.clip(r0 - win_r // 2, 0, rows - reg_r)
        kb = lax.dynamic_slice_in_dim(kg, rs, reg_r, axis=1).reshape(b, n_reg, h, d)
        vb = lax.dynamic_slice_in_dim(vg, rs, reg_r, axis=1).reshape(b, n_reg, h, d)
        q_row = r0 + q_loc // GRID_W
        k_row = rs + k_loc // GRID_W
        q_wr = jnp.clip(q_row - win_r // 2, 0, rows - win_r)
        row_ok = (k_row[None] >= q_wr[:, None]) & (k_row[None] < q_wr[:, None] + win_r)
        dr = jnp.clip(k_row[None] - q_row[:, None] + WIN_R_MAX - 1, 0, 2 * WIN_R_MAX - 2)
        bias = rpb[:, dr, dc].astype(jnp.float32)
        sc = jnp.einsum('bqhd,bkhd->bhqk', qi, kb, preferred_element_type=jnp.float32) * scale + bias[None]
        sc = jnp.where((row_ok & col_ok)[None, None], sc, NEG_INF)
        p = jax.nn.softmax(sc, axis=-1).astype(vb.dtype)
        return jnp.einsum('bhqk,bkhd->bqhd', p, vb)

    out = lax.map(one_block, (jnp.arange(nb), qb))
    return out.transpose(1, 0, 2, 3, 4).reshape(b, s, h * d)


def _memory_attn(q, k, v):
    b, s, h, d = q.shape
    sc = jnp.einsum('bqhd,bmhd->bhqm', q, k, preferred_element_type=jnp.float32) * (d ** -0.5)
    p = jax.nn.softmax(sc, axis=-1).astype(v.dtype)
    return jnp.einsum('bhqm,bmhd->bqhd', p, v).reshape(b, s, h * d)


def _layer(x, mem, w_in, q_norm, k_norm, rpb, w_mem_kv, w_branch, w_out, ln_g, ln_b):
    b, s, _ = x.shape
    hproj = x @ w_in
    qa, ka, va, za, qb, kb, vb, zb, qm, zm, gl = jnp.split(hproj, SPLIT_POINTS, axis=-1)
    cos, sin = _axial_rope_tables(s)
    qa = _apply_axial_rope(_rms_norm(qa.reshape(b, s, N_HEADS_A, HD), q_norm), cos, sin)
    ka = _apply_axial_rope(_rms_norm(ka.reshape(b, s, N_KV_A, HD), k_norm), cos, sin)
    oa = _global_gqa(qa, ka, va.reshape(b, s, N_KV_A, HD))
    ob = _neighbourhood_attn(qb.reshape(b, s, N_HEADS_B, HD), kb.reshape(b, s, N_HEADS_B, HD),
                             vb.reshape(b, s, N_HEADS_B, HD), rpb)
    km, vm = jnp.split(mem @ w_mem_kv, 2, axis=-1)
    om = _memory_attn(qm.reshape(b, s, N_HEADS_M, HD_M), km.reshape(b, -1, N_HEADS_M, HD_M),
                      vm.reshape(b, -1, N_HEADS_M, HD_M))
    branches = jnp.stack([oa * jax.nn.silu(za), ob * jax.nn.silu(zb), om * jax.nn.silu(zm)], axis=2)
    proj = jnp.einsum('bsnc,ncd->bsnd', branches, w_branch)
    gates = jax.nn.sigmoid(gl.reshape(b, s, N_BRANCH, D_MODEL))
    merged = jnp.sum(gates * proj, axis=2)
    out = merged @ w_out
    return _layer_norm(DN_ALPHA * x + out, ln_g, ln_b)


def setup_inputs(seed: int = 0) -> dict:
    key = jax.random.key(seed)
    ks = jax.random.split(key, 13)
    f32 = jnp.float32
    x_prompt = jax.random.normal(ks[0], (BATCH, SEQ, D_MODEL), f32)
    x_sample = jax.random.normal(ks[1], (DEC_BATCH, DEC_SEQ, D_MODEL), f32)
    mem_prompt = jax.random.normal(ks[2], (BATCH, N_MEM, D_MODEL), f32)
    mem_sample = jax.random.normal(ks[3], (DEC_BATCH, N_MEM, D_MODEL), f32)
    w_in = jax.random.normal(ks[4], (DEPTH, D_MODEL, D_IN), f32) * D_MODEL ** -0.5
    q_norm = 1.0 + 0.01 * jax.random.normal(ks[5], (DEPTH, HD), f32)
    k_norm = 1.0 + 0.01 * jax.random.normal(ks[6], (DEPTH, HD), f32)
    rpb = 0.02 * jax.random.normal(ks[7], (DEPTH, N_HEADS_B, 2 * WIN_R_MAX - 1, 2 * WIN_C - 1), f32)
    w_mem_kv = jax.random.normal(ks[8], (DEPTH, D_MODEL, 2 * N_HEADS_M * HD_M), f32) * D_MODEL ** -0.5
    w_branch = jax.random.normal(ks[9], (DEPTH, N_BRANCH, BRANCH_W, D_MODEL), f32) * (BRANCH_W ** -0.5 * DN_BETA)
    w_out = jax.random.normal(ks[10], (DEPTH, D_MODEL, D_MODEL), f32) * (D_MODEL ** -0.5 * DN_BETA)
    ln_g = 1.0 + 0.01 * jax.random.normal(ks[11], (DEPTH, D_MODEL), f32)
    ln_b = 0.01 * jax.random.normal(ks[12], (DEPTH, D_MODEL), f32)
    return {'x_prompt': x_prompt, 'x_sample': x_sample, 'mem_prompt': mem_prompt, 'mem_sample': mem_sample,
            'w_in': w_in, 'q_norm': q_norm, 'k_norm': k_norm, 'rpb': rpb, 'w_mem_kv': w_mem_kv,
            'w_branch': w_branch, 'w_out': w_out, 'ln_g': ln_g, 'ln_b': ln_b}


def reference(x_prompt, x_sample, mem_prompt, mem_sample, w_in, q_norm, k_norm, rpb, w_mem_kv,
              w_branch, w_out, ln_g, ln_b):
    y_prompt = x_prompt
    y_sample = x_sample
    for l in range(DEPTH):
        y_prompt = _layer(y_prompt, mem_prompt, w_in[l], q_norm[l], k_norm[l], rpb[l], w_mem_kv[l],
                          w_branch[l], w_out[l], ln_g[l], ln_b[l])
        y_sample = _layer(y_sample, mem_sample, w_in[l], q_norm[l], k_norm[l], rpb[l], w_mem_kv[l],
                          w_branch[l], w_out[l], ln_g[l], ln_b[l])
    return (y_prompt, y_sample)
```

```python
import functools

import numpy as np
import jax
import jax.numpy as jnp
from jax import lax
from jax.experimental import pallas as pl
from jax.experimental.pallas import tpu as pltpu

F32 = jnp.float32
BF16 = jnp.bfloat16

D_MODEL = 1024
GRID_W = 64
HD = 64
N_HEADS_A = 8
N_KV_A = 2
GROUP_A = N_HEADS_A // N_KV_A
N_HEADS_B = 8
N_HEADS_M = 4
HD_M = 128
WIN_R = 8
WIN_C = 16
BRANCH_W = 512
N_BRANCH = 3
ROPE_THETA = 10000.0
ROPE_FREQS = HD // 4
EPS_QK = 1e-6
EPS_LN = 1e-5
NEG_INF = -1e30

_SIZES = (N_HEADS_A * HD, N_KV_A * HD, N_KV_A * HD, BRANCH_W,
          N_HEADS_B * HD, N_HEADS_B * HD, N_HEADS_B * HD, BRANCH_W,
          N_HEADS_M * HD_M, BRANCH_W, N_BRANCH * D_MODEL)
_OFF = tuple(int(v) for v in np.cumsum((0,) + _SIZES))
D_IN = _OFF[-1]

VMEM_LIMIT = 56 * 1024 * 1024

PROJ_TM = 256
FLASH_TQ = 128
FLASH_TK = 256
NBR_QROWS = 2
NBR_KROWS = 10
EPI_TM = 256


def _sigmoid(z):
    return 1.0 / (1.0 + jnp.exp(-z))


def _norm_rope_t(h_t, g_col, cos_t, sin_t):
    ms = jnp.mean(h_t * h_t, axis=0, keepdims=True)
    y = h_t * lax.rsqrt(ms + EPS_QK) * g_col
    parts = []
    for ax in range(2):
        x1 = y[32 * ax:32 * ax + 16]
        x2 = y[32 * ax + 16:32 * ax + 32]
        c = cos_t[16 * ax:16 * ax + 16]
        s = sin_t[16 * ax:16 * ax + 16]
        parts.append(x1 * c - x2 * s)
        parts.append(x2 * c + x1 * s)
    return jnp.concatenate(parts, axis=0)


def _proj_kernel(x_ref, w_ref, cos_ref, sin_ref, gq_ref, gk_ref,
                 qat_ref, ka_ref, vat_ref, sa_ref, qbt_ref, kb_ref, vbt_ref, sb_ref,
                 qm_ref, sm_ref, sg_ref):
    xb = x_ref[...].astype(BF16)

    def seg(j):
        return jnp.dot(xb, w_ref[:, _OFF[j]:_OFF[j + 1]], preferred_element_type=F32)

    cos_t = cos_ref[...]
    sin_t = sin_ref[...]
    scale = HD ** -0.5

    qa_t = seg(0).T
    gq = gq_ref[...]
    heads = [_norm_rope_t(qa_t[HD * h:HD * (h + 1)], gq, cos_t, sin_t) * scale
             for h in range(N_HEADS_A)]
    qat_ref[...] = jnp.concatenate(heads, axis=0).astype(BF16)

    ka_t = seg(1).T
    gk = gk_ref[...]
    kheads = [_norm_rope_t(ka_t[HD * h:HD * (h + 1)], gk, cos_t, sin_t) for h in range(N_KV_A)]
    ka_ref[...] = jnp.concatenate(kheads, axis=0).T.astype(BF16)

    vat_ref[...] = seg(2).T.astype(BF16)
    za = seg(3)
    sa_ref[...] = (za * _sigmoid(za)).astype(BF16)
    qbt_ref[...] = (seg(4) * scale).T.astype(BF16)
    kb_ref[...] = seg(5).astype(BF16)
    vbt_ref[...] = seg(6).T.astype(BF16)
    zb = seg(7)
    sb_ref[...] = (zb * _sigmoid(zb)).astype(BF16)
    qm_ref[...] = seg(8).astype(BF16)
    zm = seg(9)
    sm_ref[...] = (zm * _sigmoid(zm)).astype(BF16)
    sg_ref[...] = _sigmoid(seg(10)).astype(BF16)


def _projection(x, w_in, cos_t, sin_t, gq, gk):
    b, s, d = x.shape
    tm = PROJ_TM
    assert s % tm == 0
    tok = lambda w: pl.BlockSpec((None, tm, w), lambda bi, i: (bi, i, 0))
    chan = lambda w: pl.BlockSpec((None, w, tm), lambda bi, i: (bi, 0, i))
    const = lambda shape: pl.BlockSpec(shape, lambda bi, i: (0,) * len(shape))
    nat = lambda w: jax.ShapeDtypeStruct((b, s, w), BF16)
    tr = lambda w: jax.ShapeDtypeStruct((b, w, s), BF16)
    return pl.pallas_call(
        _proj_kernel,
        grid=(b, s // tm),
        in_specs=[tok(d),
                  pl.BlockSpec((d, D_IN), lambda bi, i: (0, 0), pipeline_mode=pl.Buffered(1)),
                  pl.BlockSpec((2 * ROPE_FREQS, tm), lambda bi, i: (0, i)),
                  pl.BlockSpec((2 * ROPE_FREQS, tm), lambda bi, i: (0, i)),
                  const((HD, 1)), const((HD, 1))],
        out_specs=[chan(512), tok(128), chan(128), tok(512), chan(512), tok(512), chan(512), tok(512),
                   tok(512), tok(512), tok(3072)],
        out_shape=[tr(512), nat(128), tr(128), nat(512), tr(512), nat(512), tr(512), nat(512),
                   nat(512), nat(512), nat(3072)],
        compiler_params=pltpu.CompilerParams(
            dimension_semantics=("parallel", "parallel"), vmem_limit_bytes=VMEM_LIMIT),
        name="proj",
    )(x, w_in, cos_t, sin_t, gq, gk)


def _flash_kernel(qt_ref, k_ref, vt_ref, o_ref, *, tk, nk):
    kv = pl.program_id(1)
    tq = qt_ref.shape[1]
    ncol = GROUP_A * tq
    q4 = qt_ref[...].astype(F32)
    qcols = jnp.concatenate([q4[HD * g:HD * (g + 1)] for g in range(GROUP_A)], axis=1)
    q2 = jnp.concatenate([qcols, qcols], axis=0)
    row = lax.broadcasted_iota(jnp.int32, (N_KV_A * HD, ncol), 0)
    q_ext = jnp.where((row // HD) == kv, q2, 0.0).astype(BF16)

    def body(i, carry):
        m, l, acc = carry
        k0 = pl.multiple_of(i * tk, tk)
        s = jnp.dot(k_ref[pl.ds(k0, tk), :], q_ext, preferred_element_type=F32)
        m_new = jnp.maximum(m, jnp.max(s, axis=0, keepdims=True))
        alpha = jnp.exp(m - m_new)
        p = jnp.exp(s - m_new)
        l = alpha * l + jnp.sum(p, axis=0, keepdims=True)
        pv = jnp.dot(vt_ref[:, pl.ds(k0, tk)], p.astype(BF16), preferred_element_type=F32)
        return m_new, l, alpha * acc + pv

    m0 = jnp.full((1, ncol), NEG_INF, F32)
    l0 = jnp.zeros((1, ncol), F32)
    acc0 = jnp.zeros((HD, ncol), F32)
    _, l, acc = lax.fori_loop(0, nk, body, (m0, l0, acc0))
    o = acc / l
    stacked = jnp.concatenate([o[:, tq * g:tq * (g + 1)] for g in range(GROUP_A)], axis=0)
    o_ref[...] = stacked.T.astype(BF16)


def _global_attention(qa_t, ka, va_t):
    b, _, s = qa_t.shape
    tq, tk = FLASH_TQ, FLASH_TK
    assert s % tq == 0 and s % tk == 0
    gw = GROUP_A * HD
    return pl.pallas_call(
        functools.partial(_flash_kernel, tk=tk, nk=s // tk),
        grid=(b, N_KV_A, s // tq),
        in_specs=[pl.BlockSpec((None, gw, tq), lambda bi, kv, j: (bi, kv, j)),
                  pl.BlockSpec((None, s, N_KV_A * HD), lambda bi, kv, j: (bi, 0, 0)),
                  pl.BlockSpec((None, HD, s), lambda bi, kv, j: (bi, kv, 0))],
        out_specs=pl.BlockSpec((None, tq, gw), lambda bi, kv, j: (bi, j, kv)),
        out_shape=jax.ShapeDtypeStruct((b, s, N_HEADS_A * HD), BF16),
        compiler_params=pltpu.CompilerParams(
            dimension_semantics=("parallel", "parallel", "parallel"), vmem_limit_bytes=VMEM_LIMIT),
        name="flash",
    )(qa_t, ka, va_t)


def _nbr_plan(rows):
    qr, kr = NBR_QROWS, NBR_KROWS
    assert rows % qr == 0 and rows >= kr and rows >= WIN_R
    nb = rows // qr
    nq, nkeys = qr * GRID_W, kr * GRID_W
    q_loc, k_loc = np.arange(nq), np.arange(nkeys)
    q_col, k_col = q_loc % GRID_W, k_loc % GRID_W
    q_wc = np.clip(q_col - WIN_C // 2, 0, GRID_W - WIN_C)
    col_ok = (k_col[:, None] >= q_wc[None]) & (k_col[:, None] < q_wc[None] + WIN_C)
    dc = np.clip(k_col[:, None] - q_col[None] + WIN_C - 1, 0, 2 * WIN_C - 2)
    patterns, blk_k0, blk_pat = {}, [], []
    masks, drs = [], []
    for i in range(nb):
        r0 = i * qr
        rs = min(max(r0 - WIN_R // 2, 0), rows - kr)
        q_row = r0 + q_loc // GRID_W
        k_row = rs + k_loc // GRID_W
        q_wr = np.clip(q_row - WIN_R // 2, 0, rows - WIN_R)
        assert rs <= q_wr.min() and q_wr.max() + WIN_R <= rs + kr
        key = (rs - r0,) + tuple(int(v) for v in (q_wr - r0))
        if key not in patterns:
            patterns[key] = len(patterns)
            row_ok = (k_row[:, None] >= q_wr[None]) & (k_row[:, None] < q_wr[None] + WIN_R)
            masks.append(row_ok & col_ok)
            drs.append(np.clip(k_row[:, None] - q_row[None] + WIN_R - 1, 0, 2 * WIN_R - 2))
        blk_k0.append(rs * GRID_W)
        blk_pat.append(patterns[key])
    return (np.asarray(blk_k0, np.int32), np.asarray(blk_pat, np.int32),
            np.stack(masks), np.stack(drs), dc)


def _nbr_tables(rpb, rows):
    blk_k0, blk_pat, masks, drs, dc = _nbr_plan(rows)
    npat, nkeys, nq = masks.shape
    gathered = rpb[:, drs, np.broadcast_to(dc, drs.shape)].astype(F32)
    bias = jnp.where(jnp.asarray(masks)[None], gathered, NEG_INF)
    bias = bias.reshape(N_HEADS_B // 2, 2, npat, nkeys, nq).transpose(2, 0, 3, 1, 4)
    bias = bias.reshape(npat, N_HEADS_B // 2, nkeys, 2 * nq)
    mask01 = jnp.asarray(np.tile(masks.astype(np.float32), (1, 1, 2)))
    return jnp.asarray(blk_k0), jnp.asarray(blk_pat), bias, mask01


def _nbr_kernel(k0_ref, pat_ref, qt_ref, k_ref, vt_ref, bias_ref, mask_ref, o_ref, *, nkeys):
    i = pl.program_id(2)
    k0 = pl.multiple_of(k0_ref[i], 128)
    pat = pat_ref[i]
    nq = qt_ref.shape[1]
    q2 = qt_ref[...].astype(F32)
    qq = jnp.concatenate([q2, q2], axis=1)
    row = lax.broadcasted_iota(jnp.int32, qq.shape, 0)
    col = lax.broadcasted_iota(jnp.int32, qq.shape, 1)
    q_ext = jnp.where((row // HD) == (col // nq), qq, 0.0).astype(BF16)
    s = jnp.dot(k_ref[pl.ds(k0, nkeys), :], q_ext, preferred_element_type=F32)
    sc = s * mask_ref[pat] + bias_ref[pat]
    m = jnp.max(sc, axis=0, keepdims=True)
    p = jnp.exp(sc - m)
    l = jnp.sum(p, axis=0, keepdims=True)
    ot = jnp.dot(vt_ref[:, pl.ds(k0, nkeys)], p.astype(BF16), preferred_element_type=F32)
    ot = ot / l
    o2 = jnp.concatenate([ot[0:HD, 0:nq], ot[HD:2 * HD, nq:2 * nq]], axis=0)
    o_ref[...] = o2.T.astype(BF16)


def _nbr_attention(qb_t, kb, vb_t, tables):
    blk_k0, blk_pat, bias, mask01 = tables
    b, _, s = qb_t.shape
    nq, nkeys = NBR_QROWS * GRID_W, NBR_KROWS * GRID_W
    npat = bias.shape[0]
    hp = N_HEADS_B // 2
    grid_spec = pltpu.PrefetchScalarGridSpec(
        num_scalar_prefetch=2,
        grid=(hp, b, s // nq),
        in_specs=[pl.BlockSpec((None, 2 * HD, nq), lambda h, bi, i, *_: (bi, h, i)),
                  pl.BlockSpec((None, s, 2 * HD), lambda h, bi, i, *_: (bi, 0, h)),
                  pl.BlockSpec((None, 2 * HD, s), lambda h, bi, i, *_: (bi, h, 0)),
                  pl.BlockSpec((npat, None, nkeys, 2 * nq), lambda h, bi, i, *_: (0, h, 0, 0)),
                  pl.BlockSpec((npat, nkeys, 2 * nq), lambda h, bi, i, *_: (0, 0, 0))],
        out_specs=pl.BlockSpec((None, nq, 2 * HD), lambda h, bi, i, *_: (bi, i, h)),
    )
    return pl.pallas_call(
        functools.partial(_nbr_kernel, nkeys=nkeys),
        grid_spec=grid_spec,
        out_shape=jax.ShapeDtypeStruct((b, s, N_HEADS_B * HD), BF16),
        compiler_params=pltpu.CompilerParams(
            dimension_semantics=("parallel", "parallel", "parallel"), vmem_limit_bytes=VMEM_LIMIT),
        name="nbr",
    )(blk_k0, blk_pat, qb_t, kb, vb_t, bias, mask01)


def _memkv_kernel(mem_ref, w_ref, kmt_ref, vm_ref):
    kv = jnp.dot(mem_ref[...].astype(BF16), w_ref[...], preferred_element_type=F32)
    half = N_HEADS_M * HD_M
    kmt_ref[...] = kv[:, :half].T.astype(BF16)
    vm_ref[...] = kv[:, half:].astype(BF16)


def _memory_kv(mem, w_mem_kv):
    b, n, d = mem.shape
    half = N_HEADS_M * HD_M
    return pl.pallas_call(
        _memkv_kernel,
        grid=(b,),
        in_specs=[pl.BlockSpec((None, n, d), lambda bi: (bi, 0, 0)),
                  pl.BlockSpec((d, 2 * half), lambda bi: (0, 0))],
        out_specs=[pl.BlockSpec((None, half, n), lambda bi: (bi, 0, 0)),
                   pl.BlockSpec((None, n, half), lambda bi: (bi, 0, 0))],
        out_shape=[jax.ShapeDtypeStruct((b, half, n), BF16), jax.ShapeDtypeStruct((b, n, half), BF16)],
        compiler_params=pltpu.CompilerParams(
            dimension_semantics=("parallel",), vmem_limit_bytes=VMEM_LIMIT),
        name="memkv",
    )(mem, w_mem_kv)


def _epilogue_kernel(x_ref, oa_ref, ob_ref, sa_ref, sb_ref, sm_ref, qm_ref, sg_ref, kmt_ref, vm_ref,
                     wb_ref, wo_ref, g_ref, b_ref, y_ref, *, alpha):
    mscale = HD_M ** -0.5
    heads = []
    for h in range(N_HEADS_M):
        lo, hi = HD_M * h, HD_M * (h + 1)
        s = jnp.dot(qm_ref[:, lo:hi], kmt_ref[lo:hi, :], preferred_element_type=F32) * mscale
        m = jnp.max(s, axis=-1, keepdims=True)
        p = jnp.exp(s - m)
        l = jnp.sum(p, axis=-1, keepdims=True)
        heads.append(jnp.dot(p.astype(BF16), vm_ref[:, lo:hi], preferred_element_type=F32) / l)
    om = jnp.concatenate(heads, axis=-1)

    branches = (oa_ref[...].astype(F32) * sa_ref[...].astype(F32),
                ob_ref[...].astype(F32) * sb_ref[...].astype(F32),
                om * sm_ref[...].astype(F32))
    merged = None
    for n, g in enumerate(branches):
        proj = jnp.dot(g.astype(BF16), wb_ref[n], preferred_element_type=F32)
        term = sg_ref[:, D_MODEL * n:D_MODEL * (n + 1)].astype(F32) * proj
        merged = term if merged is None else merged + term
    out = jnp.dot(merged.astype(BF16), wo_ref[...], preferred_element_type=F32)

    r = alpha * x_ref[...] + out
    mu = jnp.mean(r, axis=-1, keepdims=True)
    c = r - mu
    var = jnp.mean(c * c, axis=-1, keepdims=True)
    y_ref[...] = c * lax.rsqrt(var + EPS_LN) * g_ref[...] + b_ref[...]


def _epilogue(x, oa, ob, sa, sb, sm, qm, sg, km_t, vm, w_branch, w_out, ln_g, ln_b, alpha):
    b, s, d = x.shape
    tm = EPI_TM
    assert s % tm == 0
    n_mem = vm.shape[1]
    half = N_HEADS_M * HD_M
    tok = lambda w: pl.BlockSpec((None, tm, w), lambda bi, i: (bi, i, 0))
    const = lambda shape: pl.BlockSpec(shape, lambda bi, i: (0,) * len(shape))
    return pl.pallas_call(
        functools.partial(_epilogue_kernel, alpha=alpha),
        grid=(b, s // tm),
        in_specs=[tok(d), tok(512), tok(512), tok(512), tok(512), tok(512), tok(512), tok(3072),
                  pl.BlockSpec((None, half, n_mem), lambda bi, i: (bi, 0, 0)),
                  pl.BlockSpec((None, n_mem, half), lambda bi, i: (bi, 0, 0)),
                  const((N_BRANCH, BRANCH_W, d)), const((d, d)), const((1, d)), const((1, d))],
        out_specs=tok(d),
        out_shape=jax.ShapeDtypeStruct((b, s, d), F32),
        compiler_params=pltpu.CompilerParams(
            dimension_semantics=("parallel", "parallel"), vmem_limit_bytes=VMEM_LIMIT),
        name="epilogue",
    )(x, oa, ob, sa, sb, sm, qm, sg, km_t, vm, w_branch, w_out, ln_g, ln_b)


def _rope_tables_t(n_tok):
    t = jnp.arange(n_tok)
    pos = jnp.stack([t // GRID_W, t % GRID_W], axis=0).astype(F32)
    inv_freq = ROPE_THETA ** (-jnp.arange(ROPE_FREQS, dtype=F32) / ROPE_FREQS)
    ang = (pos[:, None, :] * inv_freq[None, :, None]).reshape(2 * ROPE_FREQS, n_tok)
    return jnp.cos(ang), jnp.sin(ang)


def _layer(x, mem, w_in, q_norm, k_norm, rpb, w_mem_kv, w_branch, w_out, ln_g, ln_b, alpha):
    _, s, _ = x.shape
    assert s % GRID_W == 0
    cos_t, sin_t = _rope_tables_t(s)
    (qa_t, ka, va_t, sa, qb_t, kb, vb_t, sb, qm, sm, sg) = _projection(
        x, w_in.astype(BF16), cos_t, sin_t, q_norm.reshape(HD, 1), k_norm.reshape(HD, 1))
    oa = _global_attention(qa_t, ka, va_t)
    ob = _nbr_attention(qb_t, kb, vb_t, _nbr_tables(rpb, s // GRID_W))
    km_t, vm = _memory_kv(mem, w_mem_kv.astype(BF16))
    return _epilogue(x, oa, ob, sa, sb, sm, qm, sg, km_t, vm, w_branch.astype(BF16), w_out.astype(BF16),
                     ln_g.reshape(1, D_MODEL), ln_b.reshape(1, D_MODEL), alpha)


def kernel(x_prompt, x_sample, mem_prompt, mem_sample, w_in, q_norm, k_norm, rpb, w_mem_kv, w_branch, w_out,
           ln_g, ln_b):
    depth = w_in.shape[0]
    alpha = (2 * depth) ** 0.25
    y_prompt, y_sample = x_prompt, x_sample
    for l in range(depth):
        args = (w_in[l], q_norm[l], k_norm[l], rpb[l], w_mem_kv[l], w_branch[l], w_out[l], ln_g[l], ln_b[l])
        y_prompt = _layer(y_prompt, mem_prompt, *args, alpha)
        y_sample = _layer(y_sample, mem_sample, *args, alpha)
    return (y_prompt, y_sample)
```

```python
import functools

import numpy as np
import jax
import jax.numpy as jnp
from jax import lax
from jax.experimental import pallas as pl
from jax.experimental.pallas import tpu as pltpu

F32 = jnp.float32
BF16 = jnp.bfloat16

D_MODEL = 1024
GRID_W = 64
HD = 64
N_HEADS_A = 8
N_KV_A = 2
GROUP_A = N_HEADS_A // N_KV_A
N_HEADS_B = 8
N_HEADS_M = 4
HD_M = 128
WIN_R = 8
WIN_C = 16
BRANCH_W = 512
N_BRANCH = 3
ROPE_THETA = 10000.0
ROPE_FREQS = HD // 4
EPS_QK = 1e-6
EPS_LN = 1e-5
NEG_INF = -1e30
LOG2_E = 1.4426950408889634

_SIZES = (N_HEADS_A * HD, N_KV_A * HD, N_KV_A * HD, BRANCH_W,
          N_HEADS_B * HD, N_HEADS_B * HD, N_HEADS_B * HD, BRANCH_W,
          N_HEADS_M * HD_M, BRANCH_W, N_BRANCH * D_MODEL)
_OFF = tuple(int(v) for v in np.cumsum((0,) + _SIZES))
D_IN = _OFF[-1]

VMEM_LIMIT = 56 * 1024 * 1024

PROJ_TM = 256
FLASH_TQ = 128
FLASH_TK = 1024
NBR_QROWS = 2
NBR_KROWS = 10
EPI_TM = 256


def _sigmoid(z):
    return 1.0 / (1.0 + jnp.exp(-z))


def _norm_rope_t(h_t, g_col, cos_t, sin_t):
    ms = jnp.mean(h_t * h_t, axis=0, keepdims=True)
    y = h_t * lax.rsqrt(ms + EPS_QK) * g_col
    parts = []
    for ax in range(2):
        x1 = y[32 * ax:32 * ax + 16]
        x2 = y[32 * ax + 16:32 * ax + 32]
        c = cos_t[16 * ax:16 * ax + 16]
        s = sin_t[16 * ax:16 * ax + 16]
        parts.append(x1 * c - x2 * s)
        parts.append(x2 * c + x1 * s)
    return jnp.concatenate(parts, axis=0)


def _proj_kernel(x_ref, w_ref, cos_ref, sin_ref, gq_ref, gk_ref,
                 qat_ref, ka_ref, vat_ref, sa_ref, qbt_ref, kb_ref, vbt_ref, sb_ref,
                 qm_ref, sm_ref, sg_ref):
    xb = x_ref[...].astype(BF16)

    def seg(j):
        return jnp.dot(xb, w_ref[:, _OFF[j]:_OFF[j + 1]], preferred_element_type=F32)

    cos_t = cos_ref[...]
    sin_t = sin_ref[...]
    scale = HD ** -0.5

    qa_t = seg(0).T
    gq = gq_ref[...]
    heads = [_norm_rope_t(qa_t[HD * h:HD * (h + 1)], gq, cos_t, sin_t) * (scale * LOG2_E)
             for h in range(N_HEADS_A)]
    qat_ref[...] = jnp.concatenate(heads, axis=0).astype(BF16)

    ka_t = seg(1).T
    gk = gk_ref[...]
    kheads = [_norm_rope_t(ka_t[HD * h:HD * (h + 1)], gk, cos_t, sin_t) for h in range(N_KV_A)]
    ka_ref[...] = jnp.concatenate(kheads, axis=0).T.astype(BF16)

    vat_ref[...] = seg(2).T.astype(BF16)
    za = seg(3)
    sa_ref[...] = (za * _sigmoid(za)).astype(BF16)
    qbt_ref[...] = (seg(4) * scale).T.astype(BF16)
    kb_ref[...] = seg(5).astype(BF16)
    vbt_ref[...] = seg(6).T.astype(BF16)
    zb = seg(7)
    sb_ref[...] = (zb * _sigmoid(zb)).astype(BF16)
    qm_ref[...] = seg(8).astype(BF16)
    zm = seg(9)
    sm_ref[...] = (zm * _sigmoid(zm)).astype(BF16)
    sg_ref[...] = _sigmoid(seg(10)).astype(BF16)


def _projection(x, w_in, cos_t, sin_t, gq, gk):
    b, s, d = x.shape
    tm = PROJ_TM
    assert s % tm == 0
    tok = lambda w: pl.BlockSpec((None, tm, w), lambda bi, i: (bi, i, 0))
    chan = lambda w: pl.BlockSpec((None, w, tm), lambda bi, i: (bi, 0, i))
    const = lambda shape: pl.BlockSpec(shape, lambda bi, i: (0,) * len(shape))
    nat = lambda w: jax.ShapeDtypeStruct((b, s, w), BF16)
    tr = lambda w: jax.ShapeDtypeStruct((b, w, s), BF16)
    return pl.pallas_call(
        _proj_kernel,
        grid=(b, s // tm),
        in_specs=[tok(d),
                  pl.BlockSpec((d, D_IN), lambda bi, i: (0, 0), pipeline_mode=pl.Buffered(1)),
                  pl.BlockSpec((2 * ROPE_FREQS, tm), lambda bi, i: (0, i)),
                  pl.BlockSpec((2 * ROPE_FREQS, tm), lambda bi, i: (0, i)),
                  const((HD, 1)), const((HD, 1))],
        out_specs=[chan(512), tok(128), chan(128), tok(512), chan(512), tok(512), chan(512), tok(512),
                   tok(512), tok(512), tok(3072)],
        out_shape=[tr(512), nat(128), tr(128), nat(512), tr(512), nat(512), tr(512), nat(512),
                   nat(512), nat(512), nat(3072)],
        compiler_params=pltpu.CompilerParams(
            dimension_semantics=("parallel", "parallel"), vmem_limit_bytes=VMEM_LIMIT),
        name="proj",
    )(x, w_in, cos_t, sin_t, gq, gk)


def _flash_kernel(qt_ref, k_ref, vt_ref, o_ref, s_buf0, s_buf1, p_buf0, p_buf1, *, tk, nk):
    kv = pl.program_id(1)
    tq = qt_ref.shape[1]
    ncol = GROUP_A * tq
    q4 = qt_ref[...].astype(F32)
    qcols = jnp.concatenate([q4[HD * g:HD * (g + 1)] for g in range(GROUP_A)], axis=1)
    q2 = jnp.concatenate([qcols, qcols], axis=0)
    row = lax.broadcasted_iota(jnp.int32, (N_KV_A * HD, ncol), 0)
    q_ext = jnp.where((row // HD) == kv, q2, 0.0).astype(BF16)

    s_bufs = (s_buf0, s_buf1)
    p_bufs = (p_buf0, p_buf1)

    def scores(t, slot):
        k0 = pl.multiple_of(t * tk, tk)
        s = jnp.dot(k_ref[pl.ds(k0, tk), :], q_ext, preferred_element_type=F32)
        s_bufs[slot][...] = s
        return jnp.max(s, axis=0, keepdims=True)

    def softmax(slot, m, l, m_tile):
        m_new = jnp.maximum(m, m_tile)
        alpha = jnp.exp2(m - m_new)
        p = jnp.exp2(s_bufs[slot][...] - m_new)
        p_bufs[slot][...] = p.astype(BF16)
        return m_new, alpha * l + jnp.sum(p, axis=0, keepdims=True), alpha

    def weighted_values(t, slot, acc, alpha):
        k0 = pl.multiple_of(t * tk, tk)
        pv = jnp.dot(vt_ref[:, pl.ds(k0, tk)], p_bufs[slot][...], preferred_element_type=F32)
        return alpha * acc + pv

    m = jnp.full((1, ncol), NEG_INF, F32)
    l = jnp.zeros((1, ncol), F32)
    acc = jnp.zeros((HD, ncol), F32)

    m_tile = scores(0, 0)
    m_next = scores(1, 1)
    m, l, alpha = softmax(0, m, l, m_tile)

    def body(j, carry):
        m, l, acc, alpha_prev, m_tile = carry
        t = 2 * j + 1
        m_next = scores(t + 1, 0)
        m, l, alpha = softmax(1, m, l, m_tile)
        acc = weighted_values(t - 1, 0, acc, alpha_prev)
        m_tile = scores(t + 2, 1)
        m, l, alpha_prev = softmax(0, m, l, m_next)
        acc = weighted_values(t, 1, acc, alpha)
        return m, l, acc, alpha_prev, m_tile

    m, l, acc, alpha_prev, m_tile = lax.fori_loop(0, (nk - 2) // 2, body, (m, l, acc, alpha, m_next))
    m, l, alpha = softmax(1, m, l, m_tile)
    acc = weighted_values(nk - 2, 0, acc, alpha_prev)
    acc = weighted_values(nk - 1, 1, acc, alpha)

    o = acc / l
    stacked = jnp.concatenate([o[:, tq * g:tq * (g + 1)] for g in range(GROUP_A)], axis=0)
    o_ref[...] = stacked.T.astype(BF16)


def _global_attention(qa_t, ka, va_t):
    b, _, s = qa_t.shape
    tq, tk = FLASH_TQ, FLASH_TK
    assert s % tq == 0 and s % (2 * tk) == 0 and s // tk >= 4
    gw = GROUP_A * HD
    ncol = GROUP_A * tq
    return pl.pallas_call(
        functools.partial(_flash_kernel, tk=tk, nk=s // tk),
        grid=(b, N_KV_A, s // tq),
        in_specs=[pl.BlockSpec((None, gw, tq), lambda bi, kv, j: (bi, kv, j)),
                  pl.BlockSpec((None, s, N_KV_A * HD), lambda bi, kv, j: (bi, 0, 0)),
                  pl.BlockSpec((None, HD, s), lambda bi, kv, j: (bi, kv, 0))],
        out_specs=pl.BlockSpec((None, tq, gw), lambda bi, kv, j: (bi, j, kv)),
        out_shape=jax.ShapeDtypeStruct((b, s, N_HEADS_A * HD), BF16),
        scratch_shapes=[pltpu.VMEM((tk, ncol), F32), pltpu.VMEM((tk, ncol), F32),
                        pltpu.VMEM((tk, ncol), BF16), pltpu.VMEM((tk, ncol), BF16)],
        compiler_params=pltpu.CompilerParams(
            dimension_semantics=("parallel", "parallel", "parallel"), vmem_limit_bytes=VMEM_LIMIT),
        name="flash",
    )(qa_t, ka, va_t)


def _nbr_plan(rows):
    qr, kr = NBR_QROWS, NBR_KROWS
    assert rows % qr == 0 and rows >= kr and rows >= WIN_R
    nb = rows // qr
    nq, nkeys = qr * GRID_W, kr * GRID_W
    q_loc, k_loc = np.arange(nq), np.arange(nkeys)
    q_col, k_col = q_loc % GRID_W, k_loc % GRID_W
    q_wc = np.clip(q_col - WIN_C // 2, 0, GRID_W - WIN_C)
    col_ok = (k_col[:, None] >= q_wc[None]) & (k_col[:, None] < q_wc[None] + WIN_C)
    patterns, blk_k0, blk_pat = {}, [], []
    masks, drs = [], []
    for i in range(nb):
        r0 = i * qr
        rs = min(max(r0 - WIN_R // 2, 0), rows - kr)
        q_row = r0 + q_loc // GRID_W
        k_row = rs + k_loc // GRID_W
        q_wr = np.clip(q_row - WIN_R // 2, 0, rows - WIN_R)
        assert rs <= q_wr.min() and q_wr.max() + WIN_R <= rs + kr
        key = (rs - r0,) + tuple(int(v) for v in (q_wr - r0))
        if key not in patterns:
            patterns[key] = len(patterns)
            row_ok = (k_row[:, None] >= q_wr[None]) & (k_row[:, None] < q_wr[None] + WIN_R)
            masks.append(row_ok & col_ok)
            drs.append(np.clip((rs + np.arange(kr))[:, None] - (r0 + np.arange(qr))[None] + WIN_R - 1,
                               0, 2 * WIN_R - 2))
        blk_k0.append(rs * GRID_W)
        blk_pat.append(patterns[key])
    dc_grid = np.clip(np.arange(GRID_W)[:, None] - np.arange(GRID_W)[None] + WIN_C - 1, 0, 2 * WIN_C - 2)
    return (np.asarray(blk_k0, np.int32), np.asarray(blk_pat, np.int32),
            np.stack(masks), np.stack(drs), dc_grid)


def _nbr_tables(rpb, rows):
    blk_k0, blk_pat, masks, drs, dc_grid = _nbr_plan(rows)
    npat, nkeys, nq = masks.shape
    kr, qr = NBR_KROWS, NBR_QROWS
    row_sel = np.eye(2 * WIN_R - 1, dtype=np.float32)[drs.reshape(-1)]
    col_sel = np.eye(2 * WIN_C - 1, dtype=np.float32)[dc_grid.reshape(-1)].T
    hi = lax.Precision.HIGHEST
    by_row = jnp.einsum('pr,hrc->hpc', row_sel, rpb.astype(F32), precision=hi)
    full = jnp.einsum('hpc,cx->hpx', by_row, col_sel, precision=hi)
    full = full.reshape(N_HEADS_B // 2, 2, npat, kr, qr, GRID_W, GRID_W).transpose(2, 0, 3, 5, 1, 4, 6)
    full = full.reshape(npat, N_HEADS_B // 2, nkeys, 2 * nq)
    mask2 = np.tile(masks, (1, 1, 2))
    bias = jnp.where(jnp.asarray(mask2)[:, None], full, NEG_INF)
    return jnp.asarray(blk_k0), jnp.asarray(blk_pat), bias, jnp.asarray(mask2.astype(np.float32))


def _nbr_kernel(k0_ref, pat_ref, qt_ref, k_ref, vt_ref, bias_ref, mask_ref, o_ref, *, nkeys):
    i = pl.program_id(2)
    k0 = pl.multiple_of(k0_ref[i], 128)
    pat = pat_ref[i]
    nq = qt_ref.shape[1]
    q2 = qt_ref[...].astype(F32)
    qq = jnp.concatenate([q2, q2], axis=1)
    row = lax.broadcasted_iota(jnp.int32, qq.shape, 0)
    col = lax.broadcasted_iota(jnp.int32, qq.shape, 1)
    q_ext = jnp.where((row // HD) == (col // nq), qq, 0.0).astype(BF16)
    s = jnp.dot(k_ref[pl.ds(k0, nkeys), :], q_ext, preferred_element_type=F32)
    sc = s * mask_ref[pat] + bias_ref[pat]
    m = jnp.max(sc, axis=0, keepdims=True)
    p = jnp.exp(sc - m)
    l = jnp.sum(p, axis=0, keepdims=True)
    ot = jnp.dot(vt_ref[:, pl.ds(k0, nkeys)], p.astype(BF16), preferred_element_type=F32)
    ot = ot / l
    o2 = jnp.concatenate([ot[0:HD, 0:nq], ot[HD:2 * HD, nq:2 * nq]], axis=0)
    o_ref[...] = o2.T.astype(BF16)


def _nbr_attention(qb_t, kb, vb_t, tables):
    blk_k0, blk_pat, bias, mask01 = tables
    b, _, s = qb_t.shape
    nq, nkeys = NBR_QROWS * GRID_W, NBR_KROWS * GRID_W
    npat = bias.shape[0]
    hp = N_HEADS_B // 2
    grid_spec = pltpu.PrefetchScalarGridSpec(
        num_scalar_prefetch=2,
        grid=(hp, b, s // nq),
        in_specs=[pl.BlockSpec((None, 2 * HD, nq), lambda h, bi, i, *_: (bi, h, i)),
                  pl.BlockSpec((None, s, 2 * HD), lambda h, bi, i, *_: (bi, 0, h)),
                  pl.BlockSpec((None, 2 * HD, s), lambda h, bi, i, *_: (bi, h, 0)),
                  pl.BlockSpec((npat, None, nkeys, 2 * nq), lambda h, bi, i, *_: (0, h, 0, 0)),
                  pl.BlockSpec((npat, nkeys, 2 * nq), lambda h, bi, i, *_: (0, 0, 0))],
        out_specs=pl.BlockSpec((None, nq, 2 * HD), lambda h, bi, i, *_: (bi, i, h)),
    )
    return pl.pallas_call(
        functools.partial(_nbr_kernel, nkeys=nkeys),
        grid_spec=grid_spec,
        out_shape=jax.ShapeDtypeStruct((b, s, N_HEADS_B * HD), BF16),
        compiler_params=pltpu.CompilerParams(
            dimension_semantics=("parallel", "parallel", "parallel"), vmem_limit_bytes=VMEM_LIMIT),
        name="nbr",
    )(blk_k0, blk_pat, qb_t, kb, vb_t, bias, mask01)


def _memkv_kernel(mem_ref, w_ref, kmt_ref, vm_ref):
    kv = jnp.dot(mem_ref[...].astype(BF16), w_ref[...], preferred_element_type=F32)
    half = N_HEADS_M * HD_M
    kmt_ref[...] = kv[:, :half].T.astype(BF16)
    vm_ref[...] = kv[:, half:].astype(BF16)


def _memory_kv(mem, w_mem_kv):
    b, n, d = mem.shape
    half = N_HEADS_M * HD_M
    return pl.pallas_call(
        _memkv_kernel,
        grid=(b,),
        in_specs=[pl.BlockSpec((None, n, d), lambda bi: (bi, 0, 0)),
                  pl.BlockSpec((d, 2 * half), lambda bi: (0, 0))],
        out_specs=[pl.BlockSpec((None, half, n), lambda bi: (bi, 0, 0)),
                   pl.BlockSpec((None, n, half), lambda bi: (bi, 0, 0))],
        out_shape=[jax.ShapeDtypeStruct((b, half, n), BF16), jax.ShapeDtypeStruct((b, n, half), BF16)],
        compiler_params=pltpu.CompilerParams(
            dimension_semantics=("parallel",), vmem_limit_bytes=VMEM_LIMIT),
        name="memkv",
    )(mem, w_mem_kv)


def _epilogue_kernel(x_ref, oa_ref, ob_ref, sa_ref, sb_ref, sm_ref, qm_ref, sg_ref, kmt_ref, vm_ref,
                     wb_ref, wo_ref, g_ref, b_ref, y_ref, *, alpha):
    mscale = HD_M ** -0.5
    heads = []
    for h in range(N_HEADS_M):
        lo, hi = HD_M * h, HD_M * (h + 1)
        s = jnp.dot(qm_ref[:, lo:hi], kmt_ref[lo:hi, :], preferred_element_type=F32) * mscale
        m = jnp.max(s, axis=-1, keepdims=True)
        p = jnp.exp(s - m)
        l = jnp.sum(p, axis=-1, keepdims=True)
        heads.append(jnp.dot(p.astype(BF16), vm_ref[:, lo:hi], preferred_element_type=F32) / l)
    om = jnp.concatenate(heads, axis=-1)

    branches = (oa_ref[...].astype(F32) * sa_ref[...].astype(F32),
                ob_ref[...].astype(F32) * sb_ref[...].astype(F32),
                om * sm_ref[...].astype(F32))
    merged = None
    for n, g in enumerate(branches):
        proj = jnp.dot(g.astype(BF16), wb_ref[n], preferred_element_type=F32)
        term = sg_ref[:, D_MODEL * n:D_MODEL * (n + 1)].astype(F32) * proj
        merged = term if merged is None else merged + term
    out = jnp.dot(merged.astype(BF16), wo_ref[...], preferred_element_type=F32)

    r = alpha * x_ref[...] + out
    mu = jnp.mean(r, axis=-1, keepdims=True)
    c = r - mu
    var = jnp.mean(c * c, axis=-1, keepdims=True)
    y_ref[...] = c * lax.rsqrt(var + EPS_LN) * g_ref[...] + b_ref[...]


def _epilogue(x, oa, ob, sa, sb, sm, qm, sg, km_t, vm, w_branch, w_out, ln_g, ln_b, alpha):
    b, s, d = x.shape
    tm = EPI_TM
    assert s % tm == 0
    n_mem = vm.shape[1]
    half = N_HEADS_M * HD_M
    tok = lambda w: pl.BlockSpec((None, tm, w), lambda bi, i: (bi, i, 0))
    const = lambda shape: pl.BlockSpec(shape, lambda bi, i: (0,) * len(shape))
    return pl.pallas_call(
        functools.partial(_epilogue_kernel, alpha=alpha),
        grid=(b, s // tm),
        in_specs=[tok(d), tok(512), tok(512), tok(512), tok(512), tok(512), tok(512), tok(3072),
                  pl.BlockSpec((None, half, n_mem), lambda bi, i: (bi, 0, 0)),
                  pl.BlockSpec((None, n_mem, half), lambda bi, i: (bi, 0, 0)),
                  const((N_BRANCH, BRANCH_W, d)), const((d, d)), const((1, d)), const((1, d))],
        out_specs=tok(d),
        out_shape=jax.ShapeDtypeStruct((b, s, d), F32),
        compiler_params=pltpu.CompilerParams(
            dimension_semantics=("parallel", "parallel"), vmem_limit_bytes=VMEM_LIMIT),
        name="epilogue",
    )(x, oa, ob, sa, sb, sm, qm, sg, km_t, vm, w_branch, w_out, ln_g, ln_b)


def _rope_tables_t(n_tok):
    t = jnp.arange(n_tok)
    pos = jnp.stack([t // GRID_W, t % GRID_W], axis=0).astype(F32)
    inv_freq = ROPE_THETA ** (-jnp.arange(ROPE_FREQS, dtype=F32) / ROPE_FREQS)
    ang = (pos[:, None, :] * inv_freq[None, :, None]).reshape(2 * ROPE_FREQS, n_tok)
    return jnp.cos(ang), jnp.sin(ang)


def _layer(x, mem, w_in, q_norm, k_norm, rpb, w_mem_kv, w_branch, w_out, ln_g, ln_b, alpha):
    _, s, _ = x.shape
    assert s % GRID_W == 0
    cos_t, sin_t = _rope_tables_t(s)
    (qa_t, ka, va_t, sa, qb_t, kb, vb_t, sb, qm, sm, sg) = _projection(
        x, w_in.astype(BF16), cos_t, sin_t, q_norm.reshape(HD, 1), k_norm.reshape(HD, 1))
    oa = _global_attention(qa_t, ka, va_t)
    ob = _nbr_attention(qb_t, kb, vb_t, _nbr_tables(rpb, s // GRID_W))
    km_t, vm = _memory_kv(mem, w_mem_kv.astype(BF16))
    return _epilogue(x, oa, ob, sa, sb, sm, qm, sg, km_t, vm, w_branch.astype(BF16), w_out.astype(BF16),
                     ln_g.reshape(1, D_MODEL), ln_b.reshape(1, D_MODEL), alpha)


def kernel(x_prompt, x_sample, mem_prompt, mem_sample, w_in, q_norm, k_norm, rpb, w_mem_kv, w_branch, w_out,
           ln_g, ln_b):
    depth = w_in.shape[0]
    alpha = (2 * depth) ** 0.25
    y_prompt, y_sample = x_prompt, x_sample
    for l in range(depth):
        args = (w_in[l], q_norm[l], k_norm[l], rpb[l], w_mem_kv[l], w_branch[l], w_out[l], ln_g[l], ln_b[l])
        y_prompt = _layer(y_prompt, mem_prompt, *args, alpha)
        y_sample = _layer(y_sample, mem_sample, *args, alpha)
    return (y_prompt, y_sample)
```

```python
import functools

import numpy as np
import jax
import jax.numpy as jnp
from jax import lax
from jax.experimental import pallas as pl
from jax.experimental.pallas import tpu as pltpu

F32 = jnp.float32
BF16 = jnp.bfloat16

D_MODEL = 1024
GRID_W = 64
HD = 64
N_HEADS_A = 8
N_KV_A = 2
GROUP_A = N_HEADS_A // N_KV_A
N_HEADS_B = 8
N_HEADS_M = 4
HD_M = 128
WIN_R = 8
WIN_C = 16
BRANCH_W = 512
N_BRANCH = 3
ROPE_THETA = 10000.0
ROPE_FREQS = HD // 4
EPS_QK = 1e-6
EPS_LN = 1e-5
NEG_INF = -1e30
LOG2_E = 1.4426950408889634
FLASH_UNSHIFTED_MAX_SCORE = 60.0
BF16_ROUNDING_SLACK = 1.02

_SIZES = (N_HEADS_A * HD, N_KV_A * HD, N_KV_A * HD, BRANCH_W,
          N_HEADS_B * HD, N_HEADS_B * HD, N_HEADS_B * HD, BRANCH_W,
          N_HEADS_M * HD_M, BRANCH_W, N_BRANCH * D_MODEL)
_OFF = tuple(int(v) for v in np.cumsum((0,) + _SIZES))
D_IN = _OFF[-1]

VMEM_LIMIT = 56 * 1024 * 1024

PROJ_TM = 256
FLASH_TQ = 128
FLASH_TK = 1024
NBR_QROWS = 2
NBR_KROWS = 10
NBR_BLOCKS_PER_STEP = 32
EPI_TM = 256


def _sigmoid(z):
    return 1.0 / (1.0 + jnp.exp(-z))


def _norm_rope_t(h_t, g_col, cos_t, sin_t):
    ms = jnp.mean(h_t * h_t, axis=0, keepdims=True)
    y = h_t * lax.rsqrt(ms + EPS_QK) * g_col
    parts = []
    for ax in range(2):
        x1 = y[32 * ax:32 * ax + 16]
        x2 = y[32 * ax + 16:32 * ax + 32]
        c = cos_t[16 * ax:16 * ax + 16]
        s = sin_t[16 * ax:16 * ax + 16]
        parts.append(x1 * c - x2 * s)
        parts.append(x2 * c + x1 * s)
    return jnp.concatenate(parts, axis=0)


def _proj_kernel(x_ref, w_ref, cos_ref, sin_ref, gq_ref, gk_ref,
                 qat_ref, ka_ref, vat_ref, sa_ref, qbt_ref, kb_ref, vbt_ref, sb_ref,
                 qm_ref, sm_ref, sg_ref):
    xb = x_ref[...].astype(BF16)

    def seg(j):
        return jnp.dot(xb, w_ref[:, _OFF[j]:_OFF[j + 1]], preferred_element_type=F32)

    cos_t = cos_ref[...]
    sin_t = sin_ref[...]
    scale = HD ** -0.5

    qa_t = seg(0).T
    gq = gq_ref[...]
    heads = [_norm_rope_t(qa_t[HD * h:HD * (h + 1)], gq, cos_t, sin_t) * (scale * LOG2_E)
             for h in range(N_HEADS_A)]
    qat_ref[...] = jnp.concatenate(heads, axis=0).astype(BF16)

    ka_t = seg(1).T
    gk = gk_ref[...]
    kheads = [_norm_rope_t(ka_t[HD * h:HD * (h + 1)], gk, cos_t, sin_t) for h in range(N_KV_A)]
    ka_ref[...] = jnp.concatenate(kheads, axis=0).T.astype(BF16)

    vat_ref[...] = seg(2).T.astype(BF16)
    za = seg(3)
    sa_ref[...] = (za * _sigmoid(za)).astype(BF16)
    qbt_ref[...] = (seg(4) * scale).T.astype(BF16)
    kb_ref[...] = seg(5).astype(BF16)
    vbt_ref[...] = seg(6).T.astype(BF16)
    zb = seg(7)
    sb_ref[...] = (zb * _sigmoid(zb)).astype(BF16)
    qm_ref[...] = seg(8).astype(BF16)
    zm = seg(9)
    sm_ref[...] = (zm * _sigmoid(zm)).astype(BF16)
    sg_ref[...] = _sigmoid(seg(10)).astype(BF16)


def _projection(x, w_in, cos_t, sin_t, gq, gk):
    b, s, d = x.shape
    tm = PROJ_TM
    assert s % tm == 0
    tok = lambda w: pl.BlockSpec((None, tm, w), lambda bi, i: (bi, i, 0))
    chan = lambda w: pl.BlockSpec((None, w, tm), lambda bi, i: (bi, 0, i))
    const = lambda shape: pl.BlockSpec(shape, lambda bi, i: (0,) * len(shape))
    nat = lambda w: jax.ShapeDtypeStruct((b, s, w), BF16)
    tr = lambda w: jax.ShapeDtypeStruct((b, w, s), BF16)
    return pl.pallas_call(
        _proj_kernel,
        grid=(b, s // tm),
        in_specs=[tok(d),
                  pl.BlockSpec((d, D_IN), lambda bi, i: (0, 0), pipeline_mode=pl.Buffered(1)),
                  pl.BlockSpec((2 * ROPE_FREQS, tm), lambda bi, i: (0, i)),
                  pl.BlockSpec((2 * ROPE_FREQS, tm), lambda bi, i: (0, i)),
                  const((HD, 1)), const((HD, 1))],
        out_specs=[chan(512), tok(128), chan(128), tok(512), chan(512), tok(512), chan(512), tok(512),
                   tok(512), tok(512), tok(3072)],
        out_shape=[tr(512), nat(128), tr(128), nat(512), tr(512), nat(512), tr(512), nat(512),
                   nat(512), nat(512), nat(3072)],
        compiler_params=pltpu.CompilerParams(
            dimension_semantics=("parallel", "parallel"), vmem_limit_bytes=VMEM_LIMIT),
        name="proj",
    )(x, w_in, cos_t, sin_t, gq, gk)


def _flash_kernel(qt_ref, k_ref, vt_ref, o_ref, *scratch, tk, nk, stabilize):
    kv = pl.program_id(1)
    tq = qt_ref.shape[1]
    ncol = GROUP_A * tq
    q4 = qt_ref[...].astype(F32)
    qcols = jnp.concatenate([q4[HD * g:HD * (g + 1)] for g in range(GROUP_A)], axis=1)
    q2 = jnp.concatenate([qcols, qcols], axis=0)
    row = lax.broadcasted_iota(jnp.int32, (N_KV_A * HD, ncol), 0)
    q_ext = jnp.where((row // HD) == kv, q2, 0.0).astype(BF16)

    def weighted_values(t, slot, acc):
        k0 = pl.multiple_of(t * tk, tk)
        return acc + jnp.dot(vt_ref[:, pl.ds(k0, tk)], p_bufs[slot][...], preferred_element_type=F32)

    def finish(acc, l):
        o = acc / l
        stacked = jnp.concatenate([o[:, tq * g:tq * (g + 1)] for g in range(GROUP_A)], axis=0)
        o_ref[...] = stacked.T.astype(BF16)

    if not stabilize:
        p_bufs = scratch

        def probabilities(t, slot, l):
            k0 = pl.multiple_of(t * tk, tk)
            p = jnp.exp2(jnp.dot(k_ref[pl.ds(k0, tk), :], q_ext, preferred_element_type=F32))
            p_bufs[slot][...] = p.astype(BF16)
            return l + jnp.sum(p, axis=0, keepdims=True)

        l = probabilities(0, 0, jnp.zeros((1, ncol), F32))

        def fast_body(j, carry):
            l, acc = carry
            t = 2 * j + 1
            l = probabilities(t, 1, l)
            acc = weighted_values(t - 1, 0, acc)
            l = probabilities(t + 1, 0, l)
            acc = weighted_values(t, 1, acc)
            return l, acc

        l, acc = lax.fori_loop(0, (nk - 2) // 2, fast_body, (l, jnp.zeros((HD, ncol), F32)))
        l = probabilities(nk - 1, 1, l)
        acc = weighted_values(nk - 2, 0, acc)
        acc = weighted_values(nk - 1, 1, acc)
        finish(acc, l)
        return

    s_bufs = scratch[:2]
    p_bufs = scratch[2:]

    def scores(t, slot):
        k0 = pl.multiple_of(t * tk, tk)
        s = jnp.dot(k_ref[pl.ds(k0, tk), :], q_ext, preferred_element_type=F32)
        s_bufs[slot][...] = s
        return jnp.max(s, axis=0, keepdims=True)

    def softmax(slot, m, l, m_tile):
        m_new = jnp.maximum(m, m_tile)
        alpha = jnp.exp2(m - m_new)
        p = jnp.exp2(s_bufs[slot][...] - m_new)
        p_bufs[slot][...] = p.astype(BF16)
        return m_new, alpha * l + jnp.sum(p, axis=0, keepdims=True), alpha

    m = jnp.full((1, ncol), NEG_INF, F32)
    l = jnp.zeros((1, ncol), F32)
    acc = jnp.zeros((HD, ncol), F32)

    m_tile = scores(0, 0)
    m_next = scores(1, 1)
    m, l, alpha = softmax(0, m, l, m_tile)

    def body(j, carry):
        m, l, acc, alpha_prev, m_tile = carry
        t = 2 * j + 1
        m_next = scores(t + 1, 0)
        m, l, alpha = softmax(1, m, l, m_tile)
        acc = weighted_values(t - 1, 0, alpha_prev * acc)
        m_tile = scores(t + 2, 1)
        m, l, alpha_prev = softmax(0, m, l, m_next)
        acc = weighted_values(t, 1, alpha * acc)
        return m, l, acc, alpha_prev, m_tile

    m, l, acc, alpha_prev, m_tile = lax.fori_loop(0, (nk - 2) // 2, body, (m, l, acc, alpha, m_next))
    m, l, alpha = softmax(1, m, l, m_tile)
    acc = weighted_values(nk - 2, 0, alpha_prev * acc)
    acc = weighted_values(nk - 1, 1, alpha * acc)
    finish(acc, l)


def _flash_call(qa_t, ka, va_t, *, stabilize):
    b, _, s = qa_t.shape
    tq, tk = FLASH_TQ, FLASH_TK
    assert s % tq == 0 and s % (2 * tk) == 0 and s // tk >= 4
    gw = GROUP_A * HD
    ncol = GROUP_A * tq
    scratch = [pltpu.VMEM((tk, ncol), BF16), pltpu.VMEM((tk, ncol), BF16)]
    if stabilize:
        scratch = [pltpu.VMEM((tk, ncol), F32), pltpu.VMEM((tk, ncol), F32)] + scratch
    return pl.pallas_call(
        functools.partial(_flash_kernel, tk=tk, nk=s // tk, stabilize=stabilize),
        grid=(b, N_KV_A, s // tq),
        in_specs=[pl.BlockSpec((None, gw, tq), lambda bi, kv, j: (bi, kv, j)),
                  pl.BlockSpec((None, s, N_KV_A * HD), lambda bi, kv, j: (bi, 0, 0)),
                  pl.BlockSpec((None, HD, s), lambda bi, kv, j: (bi, kv, 0))],
        out_specs=pl.BlockSpec((None, tq, gw), lambda bi, kv, j: (bi, j, kv)),
        out_shape=jax.ShapeDtypeStruct((b, s, N_HEADS_A * HD), BF16),
        scratch_shapes=scratch,
        compiler_params=pltpu.CompilerParams(
            dimension_semantics=("parallel", "parallel", "parallel"), vmem_limit_bytes=VMEM_LIMIT),
        name="flash_stable" if stabilize else "flash",
    )(qa_t, ka, va_t)


def _global_attention(qa_t, ka, va_t, score_bound):
    return lax.cond(score_bound <= FLASH_UNSHIFTED_MAX_SCORE,
                    functools.partial(_flash_call, stabilize=False),
                    functools.partial(_flash_call, stabilize=True),
                    qa_t, ka, va_t)


def _nbr_plan(rows):
    qr, kr = NBR_QROWS, NBR_KROWS
    assert rows % qr == 0 and rows >= kr and rows >= WIN_R
    nb = rows // qr
    nq, nkeys = qr * GRID_W, kr * GRID_W
    q_loc, k_loc = np.arange(nq), np.arange(nkeys)
    q_col, k_col = q_loc % GRID_W, k_loc % GRID_W
    q_wc = np.clip(q_col - WIN_C // 2, 0, GRID_W - WIN_C)
    col_ok = (k_col[:, None] >= q_wc[None]) & (k_col[:, None] < q_wc[None] + WIN_C)
    patterns, blk_k0, blk_pat = {}, [], []
    masks, drs = [], []
    for i in range(nb):
        r0 = i * qr
        rs = min(max(r0 - WIN_R // 2, 0), rows - kr)
        q_row = r0 + q_loc // GRID_W
        k_row = rs + k_loc // GRID_W
        q_wr = np.clip(q_row - WIN_R // 2, 0, rows - WIN_R)
        assert rs <= q_wr.min() and q_wr.max() + WIN_R <= rs + kr
        key = (rs - r0,) + tuple(int(v) for v in (q_wr - r0))
        if key not in patterns:
            patterns[key] = len(patterns)
            row_ok = (k_row[:, None] >= q_wr[None]) & (k_row[:, None] < q_wr[None] + WIN_R)
            masks.append(row_ok & col_ok)
            drs.append(np.clip((rs + np.arange(kr))[:, None] - (r0 + np.arange(qr))[None] + WIN_R - 1,
                               0, 2 * WIN_R - 2))
        blk_k0.append(rs * GRID_W)
        blk_pat.append(patterns[key])
    dc_grid = np.clip(np.arange(GRID_W)[:, None] - np.arange(GRID_W)[None] + WIN_C - 1, 0, 2 * WIN_C - 2)
    return (np.asarray(blk_k0, np.int32), np.asarray(blk_pat, np.int32),
            np.stack(masks), np.stack(drs), dc_grid)


def _nbr_tables(rpb, rows):
    blk_k0, blk_pat, masks, drs, dc_grid = _nbr_plan(rows)
    npat, nkeys, nq = masks.shape
    kr, qr = NBR_KROWS, NBR_QROWS
    row_sel = np.eye(2 * WIN_R - 1, dtype=np.float32)[drs.reshape(-1)]
    col_sel = np.eye(2 * WIN_C - 1, dtype=np.float32)[dc_grid.reshape(-1)].T
    hi = lax.Precision.HIGHEST
    by_row = jnp.einsum('pr,hrc->hpc', row_sel, rpb.astype(F32), precision=hi)
    full = jnp.einsum('hpc,cx->hpx', by_row, col_sel, precision=hi)
    full = full.reshape(N_HEADS_B // 2, 2, npat, kr, qr, GRID_W, GRID_W).transpose(2, 0, 3, 5, 1, 4, 6)
    full = full.reshape(npat, N_HEADS_B // 2, nkeys, 2 * nq)
    mask2 = np.tile(masks, (1, 1, 2))
    bias = jnp.where(jnp.asarray(mask2)[:, None], full, NEG_INF)
    return jnp.asarray(blk_k0), jnp.asarray(blk_pat), bias, jnp.asarray(mask2.astype(np.float32))


def _nbr_kernel(k0_ref, pat_ref, qt_ref, k_ref, vt_ref, bias_ref, mask_ref, o_ref,
                s_buf0, s_buf1, p_buf0, p_buf1, *, nkeys, nq):
    i = pl.program_id(2)
    n_blk = qt_ref.shape[1] // nq
    s_bufs = (s_buf0, s_buf1)
    p_bufs = (p_buf0, p_buf1)
    row = lax.broadcasted_iota(jnp.int32, (2 * HD, 2 * nq), 0)
    col = lax.broadcasted_iota(jnp.int32, (2 * HD, 2 * nq), 1)
    pair_diag = (row // HD) == (col // nq)

    def key_start(c):
        return pl.multiple_of(k0_ref[i * n_blk + c], 128)

    def scores(c, slot):
        pat = pat_ref[i * n_blk + c]
        q0 = pl.multiple_of(c * nq, nq)
        q2 = qt_ref[:, pl.ds(q0, nq)].astype(F32)
        qq = jnp.concatenate([q2, q2], axis=1)
        q_ext = jnp.where(pair_diag, qq, 0.0).astype(BF16)
        s = jnp.dot(k_ref[pl.ds(key_start(c), nkeys), :], q_ext, preferred_element_type=F32)
        sc = s * mask_ref[pat] + bias_ref[pat]
        s_bufs[slot][...] = sc
        return jnp.max(sc, axis=0, keepdims=True)

    def softmax(slot, m):
        p = jnp.exp(s_bufs[slot][...] - m)
        p_bufs[slot][...] = p.astype(BF16)
        return jnp.sum(p, axis=0, keepdims=True)

    def output(c, slot, l):
        ot = jnp.dot(vt_ref[:, pl.ds(key_start(c), nkeys)], p_bufs[slot][...],
                     preferred_element_type=F32)
        ot = ot / l
        o2 = jnp.concatenate([ot[0:HD, 0:nq], ot[HD:2 * HD, nq:2 * nq]], axis=0)
        q0 = pl.multiple_of(c * nq, nq)
        o_ref[pl.ds(q0, nq), :] = o2.T.astype(BF16)

    m_cur = scores(0, 0)
    m_next = scores(1, 1)
    l_prev = softmax(0, m_cur)

    def body(j, carry):
        m_cur, l_prev = carry
        c = 2 * j + 1
        m_next = scores(c + 1, 0)
        l_cur = softmax(1, m_cur)
        output(c - 1, 0, l_prev)
        m_cur = scores(c + 2, 1)
        l_prev = softmax(0, m_next)
        output(c, 1, l_cur)
        return m_cur, l_prev

    m_cur, l_prev = lax.fori_loop(0, (n_blk - 2) // 2, body, (m_next, l_prev))
    l_cur = softmax(1, m_cur)
    output(n_blk - 2, 0, l_prev)
    output(n_blk - 1, 1, l_cur)


def _nbr_attention(qb_t, kb, vb_t, tables):
    blk_k0, blk_pat, bias, mask01 = tables
    b, _, s = qb_t.shape
    nq, nkeys = NBR_QROWS * GRID_W, NBR_KROWS * GRID_W
    npat = bias.shape[0]
    hp = N_HEADS_B // 2
    n_blk = min(NBR_BLOCKS_PER_STEP, s // nq)
    tq = nq * n_blk
    assert s % tq == 0 and n_blk % 2 == 0 and n_blk >= 4
    grid_spec = pltpu.PrefetchScalarGridSpec(
        num_scalar_prefetch=2,
        grid=(hp, b, s // tq),
        in_specs=[pl.BlockSpec((None, 2 * HD, tq), lambda h, bi, i, *_: (bi, h, i)),
                  pl.BlockSpec((None, s, 2 * HD), lambda h, bi, i, *_: (bi, 0, h)),
                  pl.BlockSpec((None, 2 * HD, s), lambda h, bi, i, *_: (bi, h, 0)),
                  pl.BlockSpec((npat, None, nkeys, 2 * nq), lambda h, bi, i, *_: (0, h, 0, 0)),
                  pl.BlockSpec((npat, nkeys, 2 * nq), lambda h, bi, i, *_: (0, 0, 0))],
        out_specs=pl.BlockSpec((None, tq, 2 * HD), lambda h, bi, i, *_: (bi, i, h)),
        scratch_shapes=[pltpu.VMEM((nkeys, 2 * nq), F32), pltpu.VMEM((nkeys, 2 * nq), F32),
                        pltpu.VMEM((nkeys, 2 * nq), BF16), pltpu.VMEM((nkeys, 2 * nq), BF16)],
    )
    return pl.pallas_call(
        functools.partial(_nbr_kernel, nkeys=nkeys, nq=nq),
        grid_spec=grid_spec,
        out_shape=jax.ShapeDtypeStruct((b, s, N_HEADS_B * HD), BF16),
        compiler_params=pltpu.CompilerParams(
            dimension_semantics=("parallel", "parallel", "parallel"), vmem_limit_bytes=VMEM_LIMIT),
        name="nbr",
    )(blk_k0, blk_pat, qb_t, kb, vb_t, bias, mask01)


def _memkv_kernel(mem_ref, w_ref, kmt_ref, vm_ref):
    kv = jnp.dot(mem_ref[...].astype(BF16), w_ref[...], preferred_element_type=F32)
    half = N_HEADS_M * HD_M
    kmt_ref[...] = kv[:, :half].T.astype(BF16)
    vm_ref[...] = kv[:, half:].astype(BF16)


def _memory_kv(mem, w_mem_kv):
    b, n, d = mem.shape
    half = N_HEADS_M * HD_M
    return pl.pallas_call(
        _memkv_kernel,
        grid=(b,),
        in_specs=[pl.BlockSpec((None, n, d), lambda bi: (bi, 0, 0)),
                  pl.BlockSpec((d, 2 * half), lambda bi: (0, 0))],
        out_specs=[pl.BlockSpec((None, half, n), lambda bi: (bi, 0, 0)),
                   pl.BlockSpec((None, n, half), lambda bi: (bi, 0, 0))],
        out_shape=[jax.ShapeDtypeStruct((b, half, n), BF16), jax.ShapeDtypeStruct((b, n, half), BF16)],
        compiler_params=pltpu.CompilerParams(
            dimension_semantics=("parallel",), vmem_limit_bytes=VMEM_LIMIT),
        name="memkv",
    )(mem, w_mem_kv)


def _epilogue_kernel(x_ref, oa_ref, ob_ref, sa_ref, sb_ref, sm_ref, qm_ref, sg_ref, kmt_ref, vm_ref,
                     wb_ref, wo_ref, g_ref, b_ref, y_ref, *, alpha):
    mscale = HD_M ** -0.5
    heads = []
    for h in range(N_HEADS_M):
        lo, hi = HD_M * h, HD_M * (h + 1)
        s = jnp.dot(qm_ref[:, lo:hi], kmt_ref[lo:hi, :], preferred_element_type=F32) * mscale
        m = jnp.max(s, axis=-1, keepdims=True)
        p = jnp.exp(s - m)
        l = jnp.sum(p, axis=-1, keepdims=True)
        heads.append(jnp.dot(p.astype(BF16), vm_ref[:, lo:hi], preferred_element_type=F32) / l)
    om = jnp.concatenate(heads, axis=-1)

    branches = (oa_ref[...].astype(F32) * sa_ref[...].astype(F32),
                ob_ref[...].astype(F32) * sb_ref[...].astype(F32),
                om * sm_ref[...].astype(F32))
    merged = None
    for n, g in enumerate(branches):
        proj = jnp.dot(g.astype(BF16), wb_ref[n], preferred_element_type=F32)
        term = sg_ref[:, D_MODEL * n:D_MODEL * (n + 1)].astype(F32) * proj
        merged = term if merged is None else merged + term
    out = jnp.dot(merged.astype(BF16), wo_ref[...], preferred_element_type=F32)

    r = alpha * x_ref[...] + out
    mu = jnp.mean(r, axis=-1, keepdims=True)
    c = r - mu
    var = jnp.mean(c * c, axis=-1, keepdims=True)
    y_ref[...] = c * lax.rsqrt(var + EPS_LN) * g_ref[...] + b_ref[...]


def _epilogue(x, oa, ob, sa, sb, sm, qm, sg, km_t, vm, w_branch, w_out, ln_g, ln_b, alpha):
    b, s, d = x.shape
    tm = EPI_TM
    assert s % tm == 0
    n_mem = vm.shape[1]
    half = N_HEADS_M * HD_M
    tok = lambda w: pl.BlockSpec((None, tm, w), lambda bi, i: (bi, i, 0))
    const = lambda shape: pl.BlockSpec(shape, lambda bi, i: (0,) * len(shape))
    return pl.pallas_call(
        functools.partial(_epilogue_kernel, alpha=alpha),
        grid=(b, s // tm),
        in_specs=[tok(d), tok(512), tok(512), tok(512), tok(512), tok(512), tok(512), tok(3072),
                  pl.BlockSpec((None, half, n_mem), lambda bi, i: (bi, 0, 0)),
                  pl.BlockSpec((None, n_mem, half), lambda bi, i: (bi, 0, 0)),
                  const((N_BRANCH, BRANCH_W, d)), const((d, d)), const((1, d)), const((1, d))],
        out_specs=tok(d),
        out_shape=jax.ShapeDtypeStruct((b, s, d), F32),
        compiler_params=pltpu.CompilerParams(
            dimension_semantics=("parallel", "parallel"), vmem_limit_bytes=VMEM_LIMIT),
        name="epilogue",
    )(x, oa, ob, sa, sb, sm, qm, sg, km_t, vm, w_branch, w_out, ln_g, ln_b)


def _rope_tables_t(n_tok):
    t = jnp.arange(n_tok)
    pos = jnp.stack([t // GRID_W, t % GRID_W], axis=0).astype(F32)
    inv_freq = ROPE_THETA ** (-jnp.arange(ROPE_FREQS, dtype=F32) / ROPE_FREQS)
    ang = (pos[:, None, :] * inv_freq[None, :, None]).reshape(2 * ROPE_FREQS, n_tok)
    return jnp.cos(ang), jnp.sin(ang)


def _layer(x, mem, w_in, q_norm, k_norm, rpb, w_mem_kv, w_branch, w_out, ln_g, ln_b, alpha):
    _, s, _ = x.shape
    assert s % GRID_W == 0
    cos_t, sin_t = _rope_tables_t(s)
    (qa_t, ka, va_t, sa, qb_t, kb, vb_t, sb, qm, sm, sg) = _projection(
        x, w_in.astype(BF16), cos_t, sin_t, q_norm.reshape(HD, 1), k_norm.reshape(HD, 1))
    score_bound = (LOG2_E * HD ** 0.5 * BF16_ROUNDING_SLACK) * jnp.max(jnp.abs(q_norm)) * jnp.max(jnp.abs(k_norm))
    oa = _global_attention(qa_t, ka, va_t, score_bound)
    ob = _nbr_attention(qb_t, kb, vb_t, _nbr_tables(rpb, s // GRID_W))
    km_t, vm = _memory_kv(mem, w_mem_kv.astype(BF16))
    return _epilogue(x, oa, ob, sa, sb, sm, qm, sg, km_t, vm, w_branch.astype(BF16), w_out.astype(BF16),
                     ln_g.reshape(1, D_MODEL), ln_b.reshape(1, D_MODEL), alpha)


def kernel(x_prompt, x_sample, mem_prompt, mem_sample, w_in, q_norm, k_norm, rpb, w_mem_kv, w_branch, w_out,
           ln_g, ln_b):
    depth = w_in.shape[0]
    alpha = (2 * depth) ** 0.25
    y_prompt, y_sample = x_prompt, x_sample
    for l in range(depth):
        args = (w_in[l], q_norm[l], k_norm[l], rpb[l], w_mem_kv[l], w_branch[l], w_out[l], ln_g[l], ln_b[l])
        y_prompt = _layer(y_prompt, mem_prompt, *args, alpha)
        y_sample = _layer(y_sample, mem_sample, *args, alpha)
    return (y_prompt, y_sample)
```

```python
import functools

import numpy as np
import jax
import jax.numpy as jnp
from jax import lax
from jax.experimental import pallas as pl
from jax.experimental.pallas import tpu as pltpu

F32 = jnp.float32
BF16 = jnp.bfloat16

D_MODEL = 1024
GRID_W = 64
HD = 64
N_HEADS_A = 8
N_KV_A = 2
GROUP_A = N_HEADS_A // N_KV_A
N_HEADS_B = 8
N_HEADS_M = 4
HD_M = 128
WIN_R = 8
WIN_C = 16
BRANCH_W = 512
N_BRANCH = 3
ROPE_THETA = 10000.0
ROPE_FREQS = HD // 4
EPS_QK = 1e-6
EPS_LN = 1e-5
NEG_INF = -1e30
LOG2_E = 1.4426950408889634
FLASH_UNSHIFTED_MAX_SCORE = 60.0
BF16_ROUNDING_SLACK = 1.02

_SIZES = (N_HEADS_A * HD, N_KV_A * HD, N_KV_A * HD, BRANCH_W,
          N_HEADS_B * HD, N_HEADS_B * HD, N_HEADS_B * HD, BRANCH_W,
          N_HEADS_M * HD_M, BRANCH_W, N_BRANCH * D_MODEL)
_OFF = tuple(int(v) for v in np.cumsum((0,) + _SIZES))
D_IN = _OFF[-1]

VMEM_LIMIT = 56 * 1024 * 1024

PROJ_TM = 256
FLASH_TQ = 256
FLASH_TK = 1024
FLASH_SUB = 256
NBR_QROWS = 2
NBR_KROWS = 10
NBR_BLOCKS_PER_STEP = 32
EPI_TM = 512


def _sigmoid(z):
    return 1.0 / (1.0 + jnp.exp(-z))


def _norm_rope_t(h_t, g_col, cos_t, sin_t):
    ms = jnp.mean(h_t * h_t, axis=0, keepdims=True)
    y = h_t * lax.rsqrt(ms + EPS_QK) * g_col
    parts = []
    for ax in range(2):
        x1 = y[32 * ax:32 * ax + 16]
        x2 = y[32 * ax + 16:32 * ax + 32]
        c = cos_t[16 * ax:16 * ax + 16]
        s = sin_t[16 * ax:16 * ax + 16]
        parts.append(x1 * c - x2 * s)
        parts.append(x2 * c + x1 * s)
    return jnp.concatenate(parts, axis=0)


def _proj_kernel(x_ref, w_ref, cos_ref, sin_ref, gq_ref, gk_ref,
                 qat_ref, ka_ref, vat_ref, sa_ref, qbt_ref, kb_ref, vbt_ref, sb_ref,
                 qm_ref, sm_ref, sg_ref):
    xb = x_ref[...].astype(BF16)

    def seg(j):
        return jnp.dot(xb, w_ref[:, _OFF[j]:_OFF[j + 1]], preferred_element_type=F32)

    cos_t = cos_ref[...]
    sin_t = sin_ref[...]
    scale = HD ** -0.5

    qa_t = seg(0).T
    gq = gq_ref[...]
    heads = [_norm_rope_t(qa_t[HD * h:HD * (h + 1)], gq, cos_t, sin_t) * (scale * LOG2_E)
             for h in range(N_HEADS_A)]
    qat_ref[...] = jnp.concatenate(heads, axis=0).astype(BF16)

    ka_t = seg(1).T
    gk = gk_ref[...]
    kheads = [_norm_rope_t(ka_t[HD * h:HD * (h + 1)], gk, cos_t, sin_t) for h in range(N_KV_A)]
    ka_ref[...] = jnp.concatenate(kheads, axis=0).T.astype(BF16)

    vat_ref[...] = seg(2).T.astype(BF16)
    za = seg(3)
    sa_ref[...] = (za * _sigmoid(za)).astype(BF16)
    qbt_ref[...] = (seg(4) * scale).T.astype(BF16)
    kb_ref[...] = seg(5).astype(BF16)
    vbt_ref[...] = seg(6).T.astype(BF16)
    zb = seg(7)
    sb_ref[...] = (zb * _sigmoid(zb)).astype(BF16)
    qm_ref[...] = seg(8).astype(BF16)
    zm = seg(9)
    sm_ref[...] = (zm * _sigmoid(zm)).astype(BF16)
    sg_ref[...] = _sigmoid(seg(10)).astype(BF16)


def _projection(x, w_in, cos_t, sin_t, gq, gk):
    b, s, d = x.shape
    tm = PROJ_TM
    assert s % tm == 0
    tok = lambda w: pl.BlockSpec((None, tm, w), lambda bi, i: (bi, i, 0))
    chan = lambda w: pl.BlockSpec((None, w, tm), lambda bi, i: (bi, 0, i))
    const = lambda shape: pl.BlockSpec(shape, lambda bi, i: (0,) * len(shape))
    nat = lambda w: jax.ShapeDtypeStruct((b, s, w), BF16)
    tr = lambda w: jax.ShapeDtypeStruct((b, w, s), BF16)
    return pl.pallas_call(
        _proj_kernel,
        grid=(b, s // tm),
        in_specs=[tok(d),
                  pl.BlockSpec((d, D_IN), lambda bi, i: (0, 0), pipeline_mode=pl.Buffered(1)),
                  pl.BlockSpec((2 * ROPE_FREQS, tm), lambda bi, i: (0, i)),
                  pl.BlockSpec((2 * ROPE_FREQS, tm), lambda bi, i: (0, i)),
                  const((HD, 1)), const((HD, 1))],
        out_specs=[chan(512), tok(128), chan(128), tok(512), chan(512), tok(512), chan(512), tok(512),
                   tok(512), tok(512), tok(3072)],
        out_shape=[tr(512), nat(128), tr(128), nat(512), tr(512), nat(512), tr(512), nat(512),
                   nat(512), nat(512), nat(3072)],
        compiler_params=pltpu.CompilerParams(
            dimension_semantics=("parallel", "parallel"), vmem_limit_bytes=VMEM_LIMIT),
        name="proj",
    )(x, w_in, cos_t, sin_t, gq, gk)


def _flash_kernel(qt_ref, k_ref, vt_ref, o_ref, *scratch, tk, nk, stabilize):
    kv = pl.program_id(1)
    tq = qt_ref.shape[1]
    ncol = GROUP_A * tq
    q4 = qt_ref[...].astype(F32)
    qcols = jnp.concatenate([q4[HD * g:HD * (g + 1)] for g in range(GROUP_A)], axis=1)
    q2 = jnp.concatenate([qcols, qcols], axis=0)
    row = lax.broadcasted_iota(jnp.int32, (N_KV_A * HD, ncol), 0)
    q_ext = jnp.where((row // HD) == kv, q2, 0.0).astype(BF16)

    def weighted_values(t, slot, acc):
        k0 = pl.multiple_of(t * tk, tk)
        return acc + jnp.dot(vt_ref[:, pl.ds(k0, tk)], p_bufs[slot][...], preferred_element_type=F32)

    def finish(acc, l):
        o = acc / l
        stacked = jnp.concatenate([o[:, tq * g:tq * (g + 1)] for g in range(GROUP_A)], axis=0)
        o_ref[...] = stacked.T.astype(BF16)

    if not stabilize:
        p_bufs = scratch

        sub = FLASH_SUB
        n_sub = tk // sub

        def step(t_new, slot_new, t_old, slot_old, l, acc):
            for c in range(n_sub):
                rows = slice(c * sub, (c + 1) * sub)
                if t_new is not None:
                    k0 = pl.multiple_of(t_new * tk + c * sub, sub)
                    p = jnp.exp2(jnp.dot(k_ref[pl.ds(k0, sub), :], q_ext, preferred_element_type=F32))
                    p_bufs[slot_new][rows, :] = p.astype(BF16)
                    l = l + jnp.sum(p, axis=0, keepdims=True)
                if t_old is not None:
                    v0 = pl.multiple_of(t_old * tk + c * sub, sub)
                    acc = acc + jnp.dot(vt_ref[:, pl.ds(v0, sub)], p_bufs[slot_old][rows, :],
                                        preferred_element_type=F32)
            return l, acc

        l = jnp.zeros((1, ncol), F32)
        acc = jnp.zeros((HD, ncol), F32)
        l, acc = step(0, 0, None, None, l, acc)

        def fast_body(j, carry):
            l, acc = carry
            t = 2 * j + 1
            l, acc = step(t, 1, t - 1, 0, l, acc)
            l, acc = step(t + 1, 0, t, 1, l, acc)
            return l, acc

        l, acc = lax.fori_loop(0, (nk - 2) // 2, fast_body, (l, acc))
        l, acc = step(nk - 1, 1, nk - 2, 0, l, acc)
        l, acc = step(None, None, nk - 1, 1, l, acc)
        finish(acc, l)
        return

    s_bufs = scratch[:2]
    p_bufs = scratch[2:]

    def scores(t, slot):
        k0 = pl.multiple_of(t * tk, tk)
        s = jnp.dot(k_ref[pl.ds(k0, tk), :], q_ext, preferred_element_type=F32)
        s_bufs[slot][...] = s
        return jnp.max(s, axis=0, keepdims=True)

    def softmax(slot, m, l, m_tile):
        m_new = jnp.maximum(m, m_tile)
        alpha = jnp.exp2(m - m_new)
        p = jnp.exp2(s_bufs[slot][...] - m_new)
        p_bufs[slot][...] = p.astype(BF16)
        return m_new, alpha * l + jnp.sum(p, axis=0, keepdims=True), alpha

    m = jnp.full((1, ncol), NEG_INF, F32)
    l = jnp.zeros((1, ncol), F32)
    acc = jnp.zeros((HD, ncol), F32)

    m_tile = scores(0, 0)
    m_next = scores(1, 1)
    m, l, alpha = softmax(0, m, l, m_tile)

    def body(j, carry):
        m, l, acc, alpha_prev, m_tile = carry
        t = 2 * j + 1
        m_next = scores(t + 1, 0)
        m, l, alpha = softmax(1, m, l, m_tile)
        acc = weighted_values(t - 1, 0, alpha_prev * acc)
        m_tile = scores(t + 2, 1)
        m, l, alpha_prev = softmax(0, m, l, m_next)
        acc = weighted_values(t, 1, alpha * acc)
        return m, l, acc, alpha_prev, m_tile

    m, l, acc, alpha_prev, m_tile = lax.fori_loop(0, (nk - 2) // 2, body, (m, l, acc, alpha, m_next))
    m, l, alpha = softmax(1, m, l, m_tile)
    acc = weighted_values(nk - 2, 0, alpha_prev * acc)
    acc = weighted_values(nk - 1, 1, alpha * acc)
    finish(acc, l)


def _flash_call(qa_t, ka, va_t, *, stabilize):
    b, _, s = qa_t.shape
    tq, tk = FLASH_TQ, FLASH_TK
    assert s % tq == 0 and s % (2 * tk) == 0 and s // tk >= 4
    gw = GROUP_A * HD
    ncol = GROUP_A * tq
    scratch = [pltpu.VMEM((tk, ncol), BF16), pltpu.VMEM((tk, ncol), BF16)]
    if stabilize:
        scratch = [pltpu.VMEM((tk, ncol), F32), pltpu.VMEM((tk, ncol), F32)] + scratch
    return pl.pallas_call(
        functools.partial(_flash_kernel, tk=tk, nk=s // tk, stabilize=stabilize),
        grid=(b, N_KV_A, s // tq),
        in_specs=[pl.BlockSpec((None, gw, tq), lambda bi, kv, j: (bi, kv, j)),
                  pl.BlockSpec((None, s, N_KV_A * HD), lambda bi, kv, j: (bi, 0, 0)),
                  pl.BlockSpec((None, HD, s), lambda bi, kv, j: (bi, kv, 0))],
        out_specs=pl.BlockSpec((None, tq, gw), lambda bi, kv, j: (bi, j, kv)),
        out_shape=jax.ShapeDtypeStruct((b, s, N_HEADS_A * HD), BF16),
        scratch_shapes=scratch,
        compiler_params=pltpu.CompilerParams(
            dimension_semantics=("parallel", "parallel", "parallel"), vmem_limit_bytes=VMEM_LIMIT),
        name="flash_stable" if stabilize else "flash",
    )(qa_t, ka, va_t)


def _global_attention(qa_t, ka, va_t, score_bound):
    return lax.cond(score_bound <= FLASH_UNSHIFTED_MAX_SCORE,
                    functools.partial(_flash_call, stabilize=False),
                    functools.partial(_flash_call, stabilize=True),
                    qa_t, ka, va_t)


def _nbr_plan(rows):
    qr, kr = NBR_QROWS, NBR_KROWS
    assert rows % qr == 0 and rows >= kr and rows >= WIN_R
    nb = rows // qr
    nq, nkeys = qr * GRID_W, kr * GRID_W
    q_loc, k_loc = np.arange(nq), np.arange(nkeys)
    q_col, k_col = q_loc % GRID_W, k_loc % GRID_W
    q_wc = np.clip(q_col - WIN_C // 2, 0, GRID_W - WIN_C)
    col_ok = (k_col[:, None] >= q_wc[None]) & (k_col[:, None] < q_wc[None] + WIN_C)
    patterns, blk_k0, blk_pat = {}, [], []
    masks, drs = [], []
    for i in range(nb):
        r0 = i * qr
        rs = min(max(r0 - WIN_R // 2, 0), rows - kr)
        q_row = r0 + q_loc // GRID_W
        k_row = rs + k_loc // GRID_W
        q_wr = np.clip(q_row - WIN_R // 2, 0, rows - WIN_R)
        assert rs <= q_wr.min() and q_wr.max() + WIN_R <= rs + kr
        key = (rs - r0,) + tuple(int(v) for v in (q_wr - r0))
        if key not in patterns:
            patterns[key] = len(patterns)
            row_ok = (k_row[:, None] >= q_wr[None]) & (k_row[:, None] < q_wr[None] + WIN_R)
            masks.append(row_ok & col_ok)
            drs.append(np.clip((rs + np.arange(kr))[:, None] - (r0 + np.arange(qr))[None] + WIN_R - 1,
                               0, 2 * WIN_R - 2))
        blk_k0.append(rs * GRID_W)
        blk_pat.append(patterns[key])
    dc_grid = np.clip(np.arange(GRID_W)[:, None] - np.arange(GRID_W)[None] + WIN_C - 1, 0, 2 * WIN_C - 2)
    return (np.asarray(blk_k0, np.int32), np.asarray(blk_pat, np.int32),
            np.stack(masks), np.stack(drs), dc_grid)


def _nbr_tables(rpb, rows):
    blk_k0, blk_pat, masks, drs, dc_grid = _nbr_plan(rows)
    npat, nkeys, nq = masks.shape
    kr, qr = NBR_KROWS, NBR_QROWS
    row_sel = np.eye(2 * WIN_R - 1, dtype=np.float32)[drs.reshape(-1)]
    col_sel = np.eye(2 * WIN_C - 1, dtype=np.float32)[dc_grid.reshape(-1)].T
    hi = lax.Precision.HIGHEST
    by_row = jnp.einsum('pr,hrc->hpc', row_sel, rpb.astype(F32), precision=hi)
    full = jnp.einsum('hpc,cx->hpx', by_row, col_sel, precision=hi)
    full = full.reshape(N_HEADS_B // 2, 2, npat, kr, qr, GRID_W, GRID_W).transpose(2, 0, 3, 5, 1, 4, 6)
    full = full.reshape(npat, N_HEADS_B // 2, nkeys, 2 * nq)
    mask2 = np.tile(masks, (1, 1, 2))
    bias = jnp.where(jnp.asarray(mask2)[:, None], full, NEG_INF)
    return jnp.asarray(blk_k0), jnp.asarray(blk_pat), bias, jnp.asarray(mask2.astype(np.float32))


def _nbr_kernel(k0_ref, pat_ref, qt_ref, k_ref, vt_ref, bias_ref, mask_ref, o_ref,
                s_buf0, s_buf1, p_buf0, p_buf1, *, nkeys, nq):
    i = pl.program_id(2)
    n_blk = qt_ref.shape[1] // nq
    s_bufs = (s_buf0, s_buf1)
    p_bufs = (p_buf0, p_buf1)
    row = lax.broadcasted_iota(jnp.int32, (2 * HD, 2 * nq), 0)
    col = lax.broadcasted_iota(jnp.int32, (2 * HD, 2 * nq), 1)
    pair_diag = (row // HD) == (col // nq)

    def key_start(c):
        return pl.multiple_of(k0_ref[i * n_blk + c], 128)

    def scores(c, slot):
        pat = pat_ref[i * n_blk + c]
        q0 = pl.multiple_of(c * nq, nq)
        q2 = qt_ref[:, pl.ds(q0, nq)].astype(F32)
        qq = jnp.concatenate([q2, q2], axis=1)
        q_ext = jnp.where(pair_diag, qq, 0.0).astype(BF16)
        s = jnp.dot(k_ref[pl.ds(key_start(c), nkeys), :], q_ext, preferred_element_type=F32)
        sc = s * mask_ref[pat] + bias_ref[pat]
        s_bufs[slot][...] = sc
        return jnp.max(sc, axis=0, keepdims=True)

    def softmax(slot, m):
        p = jnp.exp(s_bufs[slot][...] - m)
        p_bufs[slot][...] = p.astype(BF16)
        return jnp.sum(p, axis=0, keepdims=True)

    def output(c, slot, l):
        ot = jnp.dot(vt_ref[:, pl.ds(key_start(c), nkeys)], p_bufs[slot][...],
                     preferred_element_type=F32)
        ot = ot / l
        o2 = jnp.concatenate([ot[0:HD, 0:nq], ot[HD:2 * HD, nq:2 * nq]], axis=0)
        q0 = pl.multiple_of(c * nq, nq)
        o_ref[pl.ds(q0, nq), :] = o2.T.astype(BF16)

    m_cur = scores(0, 0)
    m_next = scores(1, 1)
    l_prev = softmax(0, m_cur)

    def body(j, carry):
        m_cur, l_prev = carry
        c = 2 * j + 1
        m_next = scores(c + 1, 0)
        l_cur = softmax(1, m_cur)
        output(c - 1, 0, l_prev)
        m_cur = scores(c + 2, 1)
        l_prev = softmax(0, m_next)
        output(c, 1, l_cur)
        return m_cur, l_prev

    m_cur, l_prev = lax.fori_loop(0, (n_blk - 2) // 2, body, (m_next, l_prev))
    l_cur = softmax(1, m_cur)
    output(n_blk - 2, 0, l_prev)
    output(n_blk - 1, 1, l_cur)


def _nbr_attention(qb_t, kb, vb_t, tables):
    blk_k0, blk_pat, bias, mask01 = tables
    b, _, s = qb_t.shape
    nq, nkeys = NBR_QROWS * GRID_W, NBR_KROWS * GRID_W
    npat = bias.shape[0]
    hp = N_HEADS_B // 2
    n_blk = min(NBR_BLOCKS_PER_STEP, s // nq)
    tq = nq * n_blk
    assert s % tq == 0 and n_blk % 2 == 0 and n_blk >= 4
    grid_spec = pltpu.PrefetchScalarGridSpec(
        num_scalar_prefetch=2,
        grid=(hp, b, s // tq),
        in_specs=[pl.BlockSpec((None, 2 * HD, tq), lambda h, bi, i, *_: (bi, h, i)),
                  pl.BlockSpec((None, s, 2 * HD), lambda h, bi, i, *_: (bi, 0, h)),
                  pl.BlockSpec((None, 2 * HD, s), lambda h, bi, i, *_: (bi, h, 0)),
                  pl.BlockSpec((npat, None, nkeys, 2 * nq), lambda h, bi, i, *_: (0, h, 0, 0)),
                  pl.BlockSpec((npat, nkeys, 2 * nq), lambda h, bi, i, *_: (0, 0, 0))],
        out_specs=pl.BlockSpec((None, tq, 2 * HD), lambda h, bi, i, *_: (bi, i, h)),
        scratch_shapes=[pltpu.VMEM((nkeys, 2 * nq), F32), pltpu.VMEM((nkeys, 2 * nq), F32),
                        pltpu.VMEM((nkeys, 2 * nq), BF16), pltpu.VMEM((nkeys, 2 * nq), BF16)],
    )
    return pl.pallas_call(
        functools.partial(_nbr_kernel, nkeys=nkeys, nq=nq),
        grid_spec=grid_spec,
        out_shape=jax.ShapeDtypeStruct((b, s, N_HEADS_B * HD), BF16),
        compiler_params=pltpu.CompilerParams(
            dimension_semantics=("parallel", "parallel", "parallel"), vmem_limit_bytes=VMEM_LIMIT),
        name="nbr",
    )(blk_k0, blk_pat, qb_t, kb, vb_t, bias, mask01)


def _memkv_kernel(mem_ref, w_ref, kmt_ref, vm_ref):
    kv = jnp.dot(mem_ref[...].astype(BF16), w_ref[...], preferred_element_type=F32)
    half = N_HEADS_M * HD_M
    kmt_ref[...] = kv[:, :half].T.astype(BF16)
    vm_ref[...] = kv[:, half:].astype(BF16)


def _memory_kv(mem, w_mem_kv):
    b, n, d = mem.shape
    half = N_HEADS_M * HD_M
    return pl.pallas_call(
        _memkv_kernel,
        grid=(b,),
        in_specs=[pl.BlockSpec((None, n, d), lambda bi: (bi, 0, 0)),
                  pl.BlockSpec((d, 2 * half), lambda bi: (0, 0))],
        out_specs=[pl.BlockSpec((None, half, n), lambda bi: (bi, 0, 0)),
                   pl.BlockSpec((None, n, half), lambda bi: (bi, 0, 0))],
        out_shape=[jax.ShapeDtypeStruct((b, half, n), BF16), jax.ShapeDtypeStruct((b, n, half), BF16)],
        compiler_params=pltpu.CompilerParams(
            dimension_semantics=("parallel",), vmem_limit_bytes=VMEM_LIMIT),
        name="memkv",
    )(mem, w_mem_kv)


def _epilogue_kernel(x_ref, oa_ref, ob_ref, sa_ref, sb_ref, sm_ref, qm_ref, sg_ref, kmt_ref, vm_ref,
                     wb_ref, wo_ref, g_ref, b_ref, y_ref, *, alpha):
    mscale = HD_M ** -0.5
    heads = []
    for h in range(N_HEADS_M):
        lo, hi = HD_M * h, HD_M * (h + 1)
        s = jnp.dot(qm_ref[:, lo:hi], kmt_ref[lo:hi, :], preferred_element_type=F32) * mscale
        m = jnp.max(s, axis=-1, keepdims=True)
        p = jnp.exp(s - m)
        l = jnp.sum(p, axis=-1, keepdims=True)
        heads.append(jnp.dot(p.astype(BF16), vm_ref[:, lo:hi], preferred_element_type=F32) / l)
    om = jnp.concatenate(heads, axis=-1)

    branches = (oa_ref[...].astype(F32) * sa_ref[...].astype(F32),
                ob_ref[...].astype(F32) * sb_ref[...].astype(F32),
                om * sm_ref[...].astype(F32))
    merged = None
    for n, g in enumerate(branches):
        proj = jnp.dot(g.astype(BF16), wb_ref[n], preferred_element_type=F32)
        term = sg_ref[:, D_MODEL * n:D_MODEL * (n + 1)].astype(F32) * proj
        merged = term if merged is None else merged + term
    out = jnp.dot(merged.astype(BF16), wo_ref[...], preferred_element_type=F32)

    r = alpha * x_ref[...] + out
    mu = jnp.mean(r, axis=-1, keepdims=True)
    c = r - mu
    var = jnp.mean(c * c, axis=-1, keepdims=True)
    y_ref[...] = c * lax.rsqrt(var + EPS_LN) * g_ref[...] + b_ref[...]


def _epilogue(x, oa, ob, sa, sb, sm, qm, sg, km_t, vm, w_branch, w_out, ln_g, ln_b, alpha):
    b, s, d = x.shape
    tm = EPI_TM
    assert s % tm == 0
    n_mem = vm.shape[1]
    half = N_HEADS_M * HD_M
    tok = lambda w: pl.BlockSpec((None, tm, w), lambda bi, i: (bi, i, 0))
    const = lambda shape: pl.BlockSpec(shape, lambda bi, i: (0,) * len(shape))
    return pl.pallas_call(
        functools.partial(_epilogue_kernel, alpha=alpha),
        grid=(b, s // tm),
        in_specs=[tok(d), tok(512), tok(512), tok(512), tok(512), tok(512), tok(512), tok(3072),
                  pl.BlockSpec((None, half, n_mem), lambda bi, i: (bi, 0, 0)),
                  pl.BlockSpec((None, n_mem, half), lambda bi, i: (bi, 0, 0)),
                  const((N_BRANCH, BRANCH_W, d)), const((d, d)), const((1, d)), const((1, d))],
        out_specs=tok(d),
        out_shape=jax.ShapeDtypeStruct((b, s, d), F32),
        compiler_params=pltpu.CompilerParams(
            dimension_semantics=("parallel", "parallel"), vmem_limit_bytes=VMEM_LIMIT),
        name="epilogue",
    )(x, oa, ob, sa, sb, sm, qm, sg, km_t, vm, w_branch, w_out, ln_g, ln_b)


def _rope_tables_t(n_tok):
    t = jnp.arange(n_tok)
    pos = jnp.stack([t // GRID_W, t % GRID_W], axis=0).astype(F32)
    inv_freq = ROPE_THETA ** (-jnp.arange(ROPE_FREQS, dtype=F32) / ROPE_FREQS)
    ang = (pos[:, None, :] * inv_freq[None, :, None]).reshape(2 * ROPE_FREQS, n_tok)
    return jnp.cos(ang), jnp.sin(ang)


def _layer(x, mem, w_in, q_norm, k_norm, rpb, w_mem_kv, w_branch, w_out, ln_g, ln_b, alpha):
    _, s, _ = x.shape
    assert s % GRID_W == 0
    cos_t, sin_t = _rope_tables_t(s)
    (qa_t, ka, va_t, sa, qb_t, kb, vb_t, sb, qm, sm, sg) = _projection(
        x, w_in.astype(BF16), cos_t, sin_t, q_norm.reshape(HD, 1), k_norm.reshape(HD, 1))
    score_bound = (LOG2_E * HD ** 0.5 * BF16_ROUNDING_SLACK) * jnp.max(jnp.abs(q_norm)) * jnp.max(jnp.abs(k_norm))
    oa = _global_attention(qa_t, ka, va_t, score_bound)
    ob = _nbr_attention(qb_t, kb, vb_t, _nbr_tables(rpb, s // GRID_W))
    km_t, vm = _memory_kv(mem, w_mem_kv.astype(BF16))
    return _epilogue(x, oa, ob, sa, sb, sm, qm, sg, km_t, vm, w_branch.astype(BF16), w_out.astype(BF16),
                     ln_g.reshape(1, D_MODEL), ln_b.reshape(1, D_MODEL), alpha)


def kernel(x_prompt, x_sample, mem_prompt, mem_sample, w_in, q_norm, k_norm, rpb, w_mem_kv, w_branch, w_out,
           ln_g, ln_b):
    depth = w_in.shape[0]
    alpha = (2 * depth) ** 0.25
    y_prompt, y_sample = x_prompt, x_sample
    for l in range(depth):
        args = (w_in[l], q_norm[l], k_norm[l], rpb[l], w_mem_kv[l], w_branch[l], w_out[l], ln_g[l], ln_b[l])
        y_prompt = _layer(y_prompt, mem_prompt, *args, alpha)
        y_sample = _layer(y_sample, mem_sample, *args, alpha)
    return (y_prompt, y_sample)
```

```python
import functools

import numpy as np
import jax
import jax.numpy as jnp
from jax import lax
from jax.experimental import pallas as pl
from jax.experimental.pallas import tpu as pltpu

F32 = jnp.float32
BF16 = jnp.bfloat16

D_MODEL = 1024
GRID_W = 64
HD = 64
N_HEADS_A = 8
N_KV_A = 2
GROUP_A = N_HEADS_A // N_KV_A
N_HEADS_B = 8
N_HEADS_M = 4
HD_M = 128
WIN_R = 8
WIN_C = 16
BRANCH_W = 512
N_BRANCH = 3
ROPE_THETA = 10000.0
ROPE_FREQS = HD // 4
EPS_QK = 1e-6
EPS_LN = 1e-5
NEG_INF = -1e30
LOG2_E = 1.4426950408889634
FLASH_UNSHIFTED_MAX_SCORE = 60.0
BF16_ROUNDING_SLACK = 1.02

_SIZES = (N_HEADS_A * HD, N_KV_A * HD, N_KV_A * HD, BRANCH_W,
          N_HEADS_B * HD, N_HEADS_B * HD, N_HEADS_B * HD, BRANCH_W,
          N_HEADS_M * HD_M, BRANCH_W, N_BRANCH * D_MODEL)
_OFF = tuple(int(v) for v in np.cumsum((0,) + _SIZES))
D_IN = _OFF[-1]

VMEM_LIMIT = 56 * 1024 * 1024

PROJ_TM = 256
FLASH_TQ = 256
FLASH_TK = 1024
FLASH_COLS = 512
FLASH_SUB = 256
NBR_QROWS = 2
NBR_KROWS = 10
NBR_BLOCKS_PER_STEP = 32
EPI_TM = 512


def _sigmoid(z):
    return 1.0 / (1.0 + jnp.exp(-z))


def _norm_rope_t(h_t, g_col, cos_t, sin_t):
    ms = jnp.mean(h_t * h_t, axis=0, keepdims=True)
    y = h_t * lax.rsqrt(ms + EPS_QK) * g_col
    parts = []
    for ax in range(2):
        x1 = y[32 * ax:32 * ax + 16]
        x2 = y[32 * ax + 16:32 * ax + 32]
        c = cos_t[16 * ax:16 * ax + 16]
        s = sin_t[16 * ax:16 * ax + 16]
        parts.append(x1 * c - x2 * s)
        parts.append(x2 * c + x1 * s)
    return jnp.concatenate(parts, axis=0)


def _proj_kernel(x_ref, w_ref, cos_ref, sin_ref, gq_ref, gk_ref,
                 qat_ref, ka_ref, vat_ref, sa_ref, qbt_ref, kb_ref, vbt_ref, sb_ref,
                 qm_ref, sm_ref, sg_ref):
    xb = x_ref[...].astype(BF16)

    def seg(j):
        return jnp.dot(xb, w_ref[:, _OFF[j]:_OFF[j + 1]], preferred_element_type=F32)

    cos_t = cos_ref[...]
    sin_t = sin_ref[...]
    scale = HD ** -0.5

    qa_t = seg(0).T
    gq = gq_ref[...]
    heads = [_norm_rope_t(qa_t[HD * h:HD * (h + 1)], gq, cos_t, sin_t) * (scale * LOG2_E)
             for h in range(N_HEADS_A)]
    qat_ref[...] = jnp.concatenate(heads, axis=0).astype(BF16)

    ka_t = seg(1).T
    gk = gk_ref[...]
    kheads = [_norm_rope_t(ka_t[HD * h:HD * (h + 1)], gk, cos_t, sin_t) for h in range(N_KV_A)]
    ka_ref[...] = jnp.concatenate(kheads, axis=0).T.astype(BF16)

    vat_ref[...] = seg(2).T.astype(BF16)
    za = seg(3)
    sa_ref[...] = (za * _sigmoid(za)).astype(BF16)
    qbt_ref[...] = (seg(4) * (scale * LOG2_E)).T.astype(BF16)
    kb_ref[...] = seg(5).astype(BF16)
    vbt_ref[...] = seg(6).T.astype(BF16)
    zb = seg(7)
    sb_ref[...] = (zb * _sigmoid(zb)).astype(BF16)
    qm_ref[...] = seg(8).astype(BF16)
    zm = seg(9)
    sm_ref[...] = (zm * _sigmoid(zm)).astype(BF16)
    sg_ref[...] = _sigmoid(seg(10)).astype(BF16)


def _projection(x, w_in, cos_t, sin_t, gq, gk):
    b, s, d = x.shape
    tm = PROJ_TM
    assert s % tm == 0
    tok = lambda w: pl.BlockSpec((None, tm, w), lambda bi, i: (bi, i, 0))
    chan = lambda w: pl.BlockSpec((None, w, tm), lambda bi, i: (bi, 0, i))
    const = lambda shape: pl.BlockSpec(shape, lambda bi, i: (0,) * len(shape))
    nat = lambda w: jax.ShapeDtypeStruct((b, s, w), BF16)
    tr = lambda w: jax.ShapeDtypeStruct((b, w, s), BF16)
    return pl.pallas_call(
        _proj_kernel,
        grid=(b, s // tm),
        in_specs=[tok(d),
                  pl.BlockSpec((d, D_IN), lambda bi, i: (0, 0), pipeline_mode=pl.Buffered(1)),
                  pl.BlockSpec((2 * ROPE_FREQS, tm), lambda bi, i: (0, i)),
                  pl.BlockSpec((2 * ROPE_FREQS, tm), lambda bi, i: (0, i)),
                  const((HD, 1)), const((HD, 1))],
        out_specs=[chan(512), tok(128), chan(128), tok(512), chan(512), tok(512), chan(512), tok(512),
                   tok(512), tok(512), tok(3072)],
        out_shape=[tr(512), nat(128), tr(128), nat(512), tr(512), nat(512), tr(512), nat(512),
                   nat(512), nat(512), nat(3072)],
        compiler_params=pltpu.CompilerParams(
            dimension_semantics=("parallel", "parallel"), vmem_limit_bytes=VMEM_LIMIT),
        name="proj",
    )(x, w_in, cos_t, sin_t, gq, gk)


def _flash_kernel(qt_ref, k_ref, vt_ref, o_ref, *scratch, tk, nk, stabilize):
    kv = pl.program_id(1)
    tq = qt_ref.shape[1]
    ncol = GROUP_A * tq
    q4 = qt_ref[...].astype(F32)
    qcols = jnp.concatenate([q4[HD * g:HD * (g + 1)] for g in range(GROUP_A)], axis=1)
    q2 = jnp.concatenate([qcols, qcols], axis=0)
    row = lax.broadcasted_iota(jnp.int32, (N_KV_A * HD, ncol), 0)
    q_ext = jnp.where((row // HD) == kv, q2, 0.0).astype(BF16)

    def weighted_values(t, slot, acc):
        k0 = pl.multiple_of(t * tk, tk)
        return acc + jnp.dot(vt_ref[:, pl.ds(k0, tk)], p_bufs[slot][...], preferred_element_type=F32)

    def finish(acc, l):
        o = acc / l
        stacked = jnp.concatenate([o[:, tq * g:tq * (g + 1)] for g in range(GROUP_A)], axis=0)
        o_ref[...] = stacked.T.astype(BF16)

    if not stabilize:
        p_bufs = scratch

        sub = FLASH_SUB
        n_sub = tk // sub

        def step(t_new, slot_new, t_old, slot_old, l, acc):
            cw = FLASH_COLS
            l_parts = [l[:, cw * j:cw * (j + 1)] for j in range(ncol // cw)]
            acc_parts = [acc[:, cw * j:cw * (j + 1)] for j in range(ncol // cw)]
            for c in range(n_sub):
                rows = slice(c * sub, (c + 1) * sub)
                for j in range(ncol // cw):
                    cols = slice(cw * j, cw * (j + 1))
                    if t_new is not None:
                        k0 = pl.multiple_of(t_new * tk + c * sub, sub)
                        p = jnp.exp2(jnp.dot(k_ref[pl.ds(k0, sub), :], q_ext[:, cols],
                                             preferred_element_type=F32))
                        p_bufs[slot_new][rows, cols] = p.astype(BF16)
                        l_parts[j] = l_parts[j] + jnp.sum(p, axis=0, keepdims=True)
                    if t_old is not None:
                        v0 = pl.multiple_of(t_old * tk + c * sub, sub)
                        acc_parts[j] = acc_parts[j] + jnp.dot(
                            vt_ref[:, pl.ds(v0, sub)], p_bufs[slot_old][rows, cols], preferred_element_type=F32)
            return jnp.concatenate(l_parts, axis=1), jnp.concatenate(acc_parts, axis=1)

        l = jnp.zeros((1, ncol), F32)
        acc = jnp.zeros((HD, ncol), F32)
        l, acc = step(0, 0, None, None, l, acc)

        def fast_body(j, carry):
            l, acc = carry
            t = 2 * j + 1
            l, acc = step(t, 1, t - 1, 0, l, acc)
            l, acc = step(t + 1, 0, t, 1, l, acc)
            return l, acc

        l, acc = lax.fori_loop(0, (nk - 2) // 2, fast_body, (l, acc))
        l, acc = step(nk - 1, 1, nk - 2, 0, l, acc)
        l, acc = step(None, None, nk - 1, 1, l, acc)
        finish(acc, l)
        return

    s_bufs = scratch[:2]
    p_bufs = scratch[2:]

    def scores(t, slot):
        k0 = pl.multiple_of(t * tk, tk)
        s = jnp.dot(k_ref[pl.ds(k0, tk), :], q_ext, preferred_element_type=F32)
        s_bufs[slot][...] = s
        return jnp.max(s, axis=0, keepdims=True)

    def softmax(slot, m, l, m_tile):
        m_new = jnp.maximum(m, m_tile)
        alpha = jnp.exp2(m - m_new)
        p = jnp.exp2(s_bufs[slot][...] - m_new)
        p_bufs[slot][...] = p.astype(BF16)
        return m_new, alpha * l + jnp.sum(p, axis=0, keepdims=True), alpha

    m = jnp.full((1, ncol), NEG_INF, F32)
    l = jnp.zeros((1, ncol), F32)
    acc = jnp.zeros((HD, ncol), F32)

    m_tile = scores(0, 0)
    m_next = scores(1, 1)
    m, l, alpha = softmax(0, m, l, m_tile)

    def body(j, carry):
        m, l, acc, alpha_prev, m_tile = carry
        t = 2 * j + 1
        m_next = scores(t + 1, 0)
        m, l, alpha = softmax(1, m, l, m_tile)
        acc = weighted_values(t - 1, 0, alpha_prev * acc)
        m_tile = scores(t + 2, 1)
        m, l, alpha_prev = softmax(0, m, l, m_next)
        acc = weighted_values(t, 1, alpha * acc)
        return m, l, acc, alpha_prev, m_tile

    m, l, acc, alpha_prev, m_tile = lax.fori_loop(0, (nk - 2) // 2, body, (m, l, acc, alpha, m_next))
    m, l, alpha = softmax(1, m, l, m_tile)
    acc = weighted_values(nk - 2, 0, alpha_prev * acc)
    acc = weighted_values(nk - 1, 1, alpha * acc)
    finish(acc, l)


def _flash_call(qa_t, ka, va_t, *, stabilize):
    b, _, s = qa_t.shape
    tq, tk = FLASH_TQ, FLASH_TK
    assert s % tq == 0 and s % (2 * tk) == 0 and s // tk >= 4
    gw = GROUP_A * HD
    ncol = GROUP_A * tq
    scratch = [pltpu.VMEM((tk, ncol), BF16), pltpu.VMEM((tk, ncol), BF16)]
    if stabilize:
        scratch = [pltpu.VMEM((tk, ncol), F32), pltpu.VMEM((tk, ncol), F32)] + scratch
    return pl.pallas_call(
        functools.partial(_flash_kernel, tk=tk, nk=s // tk, stabilize=stabilize),
        grid=(b, N_KV_A, s // tq),
        in_specs=[pl.BlockSpec((None, gw, tq), lambda bi, kv, j: (bi, kv, j)),
                  pl.BlockSpec((None, s, N_KV_A * HD), lambda bi, kv, j: (bi, 0, 0)),
                  pl.BlockSpec((None, HD, s), lambda bi, kv, j: (bi, kv, 0))],
        out_specs=pl.BlockSpec((None, tq, gw), lambda bi, kv, j: (bi, j, kv)),
        out_shape=jax.ShapeDtypeStruct((b, s, N_HEADS_A * HD), BF16),
        scratch_shapes=scratch,
        compiler_params=pltpu.CompilerParams(
            dimension_semantics=("parallel", "parallel", "parallel"), vmem_limit_bytes=VMEM_LIMIT),
        name="flash_stable" if stabilize else "flash",
    )(qa_t, ka, va_t)


def _global_attention(qa_t, ka, va_t, score_bound):
    return lax.cond(score_bound <= FLASH_UNSHIFTED_MAX_SCORE,
                    functools.partial(_flash_call, stabilize=False),
                    functools.partial(_flash_call, stabilize=True),
                    qa_t, ka, va_t)


def _nbr_plan(rows):
    qr, kr = NBR_QROWS, NBR_KROWS
    assert rows % qr == 0 and rows >= kr and rows >= WIN_R
    nb = rows // qr
    nq, nkeys = qr * GRID_W, kr * GRID_W
    q_loc, k_loc = np.arange(nq), np.arange(nkeys)
    q_col, k_col = q_loc % GRID_W, k_loc % GRID_W
    q_wc = np.clip(q_col - WIN_C // 2, 0, GRID_W - WIN_C)
    col_ok = (k_col[:, None] >= q_wc[None]) & (k_col[:, None] < q_wc[None] + WIN_C)
    patterns, blk_k0, blk_pat = {}, [], []
    masks, drs = [], []
    for i in range(nb):
        r0 = i * qr
        rs = min(max(r0 - WIN_R // 2, 0), rows - kr)
        q_row = r0 + q_loc // GRID_W
        k_row = rs + k_loc // GRID_W
        q_wr = np.clip(q_row - WIN_R // 2, 0, rows - WIN_R)
        assert rs <= q_wr.min() and q_wr.max() + WIN_R <= rs + kr
        key = (rs - r0,) + tuple(int(v) for v in (q_wr - r0))
        if key not in patterns:
            patterns[key] = len(patterns)
            row_ok = (k_row[:, None] >= q_wr[None]) & (k_row[:, None] < q_wr[None] + WIN_R)
            masks.append(row_ok & col_ok)
            drs.append(np.clip((rs + np.arange(kr))[:, None] - (r0 + np.arange(qr))[None] + WIN_R - 1,
                               0, 2 * WIN_R - 2))
        blk_k0.append(rs * GRID_W)
        blk_pat.append(patterns[key])
    dc_grid = np.clip(np.arange(GRID_W)[:, None] - np.arange(GRID_W)[None] + WIN_C - 1, 0, 2 * WIN_C - 2)
    return (np.asarray(blk_k0, np.int32), np.asarray(blk_pat, np.int32),
            np.stack(masks), np.stack(drs), dc_grid)


def _nbr_tables(rpb, rows):
    blk_k0, blk_pat, masks, drs, dc_grid = _nbr_plan(rows)
    npat, nkeys, nq = masks.shape
    kr, qr = NBR_KROWS, NBR_QROWS
    row_sel = np.eye(2 * WIN_R - 1, dtype=np.float32)[drs.reshape(-1)]
    col_sel = np.eye(2 * WIN_C - 1, dtype=np.float32)[dc_grid.reshape(-1)].T
    hi = lax.Precision.HIGHEST
    by_row = jnp.einsum('pr,hrc->hpc', row_sel, rpb.astype(F32), precision=hi)
    full = jnp.einsum('hpc,cx->hpx', by_row, col_sel, precision=hi)
    full = full.reshape(N_HEADS_B // 2, 2, npat, kr, qr, GRID_W, GRID_W).transpose(2, 0, 3, 5, 1, 4, 6)
    full = full.reshape(npat, N_HEADS_B // 2, nkeys, 2 * nq)
    mask2 = np.tile(masks, (1, 1, 2))
    bias = jnp.where(jnp.asarray(mask2)[:, None], full * LOG2_E, NEG_INF)
    return jnp.asarray(blk_k0), jnp.asarray(blk_pat), bias


def _nbr_kernel(k0_ref, pat_ref, qt_ref, k_ref, vt_ref, bias_ref, o_ref,
                s_buf0, s_buf1, p_buf0, p_buf1, *, nkeys, nq):
    i = pl.program_id(2)
    n_blk = qt_ref.shape[1] // nq
    s_bufs = (s_buf0, s_buf1)
    p_bufs = (p_buf0, p_buf1)
    row = lax.broadcasted_iota(jnp.int32, (2 * HD, 2 * nq), 0)
    col = lax.broadcasted_iota(jnp.int32, (2 * HD, 2 * nq), 1)
    pair_diag = (row // HD) == (col // nq)

    def key_start(c):
        return pl.multiple_of(k0_ref[i * n_blk + c], 128)

    def scores(c, slot):
        pat = pat_ref[i * n_blk + c]
        q0 = pl.multiple_of(c * nq, nq)
        q2 = qt_ref[:, pl.ds(q0, nq)].astype(F32)
        qq = jnp.concatenate([q2, q2], axis=1)
        q_ext = jnp.where(pair_diag, qq, 0.0).astype(BF16)
        s = jnp.dot(k_ref[pl.ds(key_start(c), nkeys), :], q_ext, preferred_element_type=F32)
        sc = s + bias_ref[pat]
        s_bufs[slot][...] = sc
        return jnp.max(sc, axis=0, keepdims=True)

    def softmax(slot, m):
        p = jnp.exp2(s_bufs[slot][...] - m)
        p_bufs[slot][...] = p.astype(BF16)
        return jnp.sum(p, axis=0, keepdims=True)

    def output(c, slot, l):
        ot = jnp.dot(vt_ref[:, pl.ds(key_start(c), nkeys)], p_bufs[slot][...],
                     preferred_element_type=F32)
        ot = ot / l
        o2 = jnp.concatenate([ot[0:HD, 0:nq], ot[HD:2 * HD, nq:2 * nq]], axis=0)
        q0 = pl.multiple_of(c * nq, nq)
        o_ref[pl.ds(q0, nq), :] = o2.T.astype(BF16)

    m0 = scores(0, 0)
    m1 = scores(1, 1)
    l0 = softmax(0, m0)
    m0 = scores(2, 0)
    l1 = softmax(1, m1)
    m1 = scores(3, 1)

    def body(j, carry):
        m0, m1, l0, l1 = carry
        c = 2 * j
        output(c - 2, 0, l0)
        l0 = softmax(0, m0)
        m0 = scores(c + 2, 0)
        output(c - 1, 1, l1)
        l1 = softmax(1, m1)
        m1 = scores(c + 3, 1)
        return m0, m1, l0, l1

    m0, m1, l0, l1 = lax.fori_loop(1, n_blk // 2 - 1, body, (m0, m1, l0, l1))
    output(n_blk - 4, 0, l0)
    l0 = softmax(0, m0)
    output(n_blk - 3, 1, l1)
    l1 = softmax(1, m1)
    output(n_blk - 2, 0, l0)
    output(n_blk - 1, 1, l1)


def _nbr_attention(qb_t, kb, vb_t, tables):
    blk_k0, blk_pat, bias = tables
    b, _, s = qb_t.shape
    nq, nkeys = NBR_QROWS * GRID_W, NBR_KROWS * GRID_W
    npat = bias.shape[0]
    hp = N_HEADS_B // 2
    n_blk = min(NBR_BLOCKS_PER_STEP, s // nq)
    tq = nq * n_blk
    assert s % tq == 0 and n_blk % 2 == 0 and n_blk >= 4
    grid_spec = pltpu.PrefetchScalarGridSpec(
        num_scalar_prefetch=2,
        grid=(hp, b, s // tq),
        in_specs=[pl.BlockSpec((None, 2 * HD, tq), lambda h, bi, i, *_: (bi, h, i)),
                  pl.BlockSpec((None, s, 2 * HD), lambda h, bi, i, *_: (bi, 0, h)),
                  pl.BlockSpec((None, 2 * HD, s), lambda h, bi, i, *_: (bi, h, 0)),
                  pl.BlockSpec((npat, None, nkeys, 2 * nq), lambda h, bi, i, *_: (0, h, 0, 0))],
        out_specs=pl.BlockSpec((None, tq, 2 * HD), lambda h, bi, i, *_: (bi, i, h)),
        scratch_shapes=[pltpu.VMEM((nkeys, 2 * nq), F32), pltpu.VMEM((nkeys, 2 * nq), F32),
                        pltpu.VMEM((nkeys, 2 * nq), BF16), pltpu.VMEM((nkeys, 2 * nq), BF16)],
    )
    return pl.pallas_call(
        functools.partial(_nbr_kernel, nkeys=nkeys, nq=nq),
        grid_spec=grid_spec,
        out_shape=jax.ShapeDtypeStruct((b, s, N_HEADS_B * HD), BF16),
        compiler_params=pltpu.CompilerParams(
            dimension_semantics=("parallel", "parallel", "parallel"), vmem_limit_bytes=VMEM_LIMIT),
        name="nbr",
    )(blk_k0, blk_pat, qb_t, kb, vb_t, bias)


def _memkv_kernel(mem_ref, w_ref, kmt_ref, vm_ref):
    kv = jnp.dot(mem_ref[...].astype(BF16), w_ref[...], preferred_element_type=F32)
    half = N_HEADS_M * HD_M
    kmt_ref[...] = kv[:, :half].T.astype(BF16)
    vm_ref[...] = kv[:, half:].astype(BF16)


def _memory_kv(mem, w_mem_kv):
    b, n, d = mem.shape
    half = N_HEADS_M * HD_M
    return pl.pallas_call(
        _memkv_kernel,
        grid=(b,),
        in_specs=[pl.BlockSpec((None, n, d), lambda bi: (bi, 0, 0)),
                  pl.BlockSpec((d, 2 * half), lambda bi: (0, 0))],
        out_specs=[pl.BlockSpec((None, half, n), lambda bi: (bi, 0, 0)),
                   pl.BlockSpec((None, n, half), lambda bi: (bi, 0, 0))],
        out_shape=[jax.ShapeDtypeStruct((b, half, n), BF16), jax.ShapeDtypeStruct((b, n, half), BF16)],
        compiler_params=pltpu.CompilerParams(
            dimension_semantics=("parallel",), vmem_limit_bytes=VMEM_LIMIT),
        name="memkv",
    )(mem, w_mem_kv)


def _epilogue_kernel(x_ref, oa_ref, ob_ref, sa_ref, sb_ref, sm_ref, qm_ref, sg_ref, kmt_ref, vm_ref,
                     wb_ref, wo_ref, g_ref, b_ref, y_ref, *, alpha):
    mscale = HD_M ** -0.5
    heads = []
    for h in range(N_HEADS_M):
        lo, hi = HD_M * h, HD_M * (h + 1)
        s = jnp.dot(qm_ref[:, lo:hi], kmt_ref[lo:hi, :], preferred_element_type=F32) * mscale
        m = jnp.max(s, axis=-1, keepdims=True)
        p = jnp.exp(s - m)
        l = jnp.sum(p, axis=-1, keepdims=True)
        heads.append(jnp.dot(p.astype(BF16), vm_ref[:, lo:hi], preferred_element_type=F32) / l)
    om = jnp.concatenate(heads, axis=-1)

    branches = (oa_ref[...].astype(F32) * sa_ref[...].astype(F32),
                ob_ref[...].astype(F32) * sb_ref[...].astype(F32),
                om * sm_ref[...].astype(F32))
    merged = None
    for n, g in enumerate(branches):
        proj = jnp.dot(g.astype(BF16), wb_ref[n], preferred_element_type=F32)
        term = sg_ref[:, D_MODEL * n:D_MODEL * (n + 1)].astype(F32) * proj
        merged = term if merged is None else merged + term
    out = jnp.dot(merged.astype(BF16), wo_ref[...], preferred_element_type=F32)

    r = alpha * x_ref[...] + out
    mu = jnp.mean(r, axis=-1, keepdims=True)
    c = r - mu
    var = jnp.mean(c * c, axis=-1, keepdims=True)
    y_ref[...] = c * lax.rsqrt(var + EPS_LN) * g_ref[...] + b_ref[...]


def _epilogue(x, oa, ob, sa, sb, sm, qm, sg, km_t, vm, w_branch, w_out, ln_g, ln_b, alpha):
    b, s, d = x.shape
    tm = EPI_TM
    assert s % tm == 0
    n_mem = vm.shape[1]
    half = N_HEADS_M * HD_M
    tok = lambda w: pl.BlockSpec((None, tm, w), lambda bi, i: (bi, i, 0))
    const = lambda shape: pl.BlockSpec(shape, lambda bi, i: (0,) * len(shape))
    return pl.pallas_call(
        functools.partial(_epilogue_kernel, alpha=alpha),
        grid=(b, s // tm),
        in_specs=[tok(d), tok(512), tok(512), tok(512), tok(512), tok(512), tok(512), tok(3072),
                  pl.BlockSpec((None, half, n_mem), lambda bi, i: (bi, 0, 0)),
                  pl.BlockSpec((None, n_mem, half), lambda bi, i: (bi, 0, 0)),
                  const((N_BRANCH, BRANCH_W, d)), const((d, d)), const((1, d)), const((1, d))],
        out_specs=tok(d),
        out_shape=jax.ShapeDtypeStruct((b, s, d), F32),
        compiler_params=pltpu.CompilerParams(
            dimension_semantics=("parallel", "parallel"), vmem_limit_bytes=VMEM_LIMIT),
        name="epilogue",
    )(x, oa, ob, sa, sb, sm, qm, sg, km_t, vm, w_branch, w_out, ln_g, ln_b)


def _rope_tables_t(n_tok):
    t = jnp.arange(n_tok)
    pos = jnp.stack([t // GRID_W, t % GRID_W], axis=0).astype(F32)
    inv_freq = ROPE_THETA ** (-jnp.arange(ROPE_FREQS, dtype=F32) / ROPE_FREQS)
    ang = (pos[:, None, :] * inv_freq[None, :, None]).reshape(2 * ROPE_FREQS, n_tok)
    return jnp.cos(ang), jnp.sin(ang)


def _layer(x, mem, w_in, q_norm, k_norm, rpb, w_mem_kv, w_branch, w_out, ln_g, ln_b, alpha):
    _, s, _ = x.shape
    assert s % GRID_W == 0
    cos_t, sin_t = _rope_tables_t(s)
    (qa_t, ka, va_t, sa, qb_t, kb, vb_t, sb, qm, sm, sg) = _projection(
        x, w_in.astype(BF16), cos_t, sin_t, q_norm.reshape(HD, 1), k_norm.reshape(HD, 1))
    score_bound = (LOG2_E * HD ** 0.5 * BF16_ROUNDING_SLACK) * jnp.max(jnp.abs(q_norm)) * jnp.max(jnp.abs(k_norm))
    oa = _global_attention(qa_t, ka, va_t, score_bound)
    ob = _nbr_attention(qb_t, kb, vb_t, _nbr_tables(rpb, s // GRID_W))
    km_t, vm = _memory_kv(mem, w_mem_kv.astype(BF16))
    return _epilogue(x, oa, ob, sa, sb, sm, qm, sg, km_t, vm, w_branch.astype(BF16), w_out.astype(BF16),
                     ln_g.reshape(1, D_MODEL), ln_b.reshape(1, D_MODEL), alpha)


def kernel(x_prompt, x_sample, mem_prompt, mem_sample, w_in, q_norm, k_norm, rpb, w_mem_kv, w_branch, w_out,
           ln_g, ln_b):
    depth = w_in.shape[0]
    alpha = (2 * depth) ** 0.25
    y_prompt, y_sample = x_prompt, x_sample
    for l in range(depth):
        args = (w_in[l], q_norm[l], k_norm[l], rpb[l], w_mem_kv[l], w_branch[l], w_out[l], ln_g[l], ln_b[l])
        y_prompt = _layer(y_prompt, mem_prompt, *args, alpha)
        y_sample = _layer(y_sample, mem_sample, *args, alpha)
    return (y_prompt, y_sample)
```

```python
import functools

import numpy as np
import jax
import jax.numpy as jnp
from jax import lax
from jax.experimental import pallas as pl
from jax.experimental.pallas import tpu as pltpu

F32 = jnp.float32
BF16 = jnp.bfloat16

D_MODEL = 1024
GRID_W = 64
HD = 64
N_HEADS_A = 8
N_KV_A = 2
GROUP_A = N_HEADS_A // N_KV_A
N_HEADS_B = 8
N_HEADS_M = 4
HD_M = 128
WIN_R = 8
WIN_C = 16
BRANCH_W = 512
N_BRANCH = 3
ROPE_THETA = 10000.0
ROPE_FREQS = HD // 4
EPS_QK = 1e-6
EPS_LN = 1e-5
NEG_INF = -1e30
LOG2_E = 1.4426950408889634
FLASH_UNSHIFTED_MAX_SCORE = 60.0
BF16_ROUNDING_SLACK = 1.02

_SIZES = (N_HEADS_A * HD, N_KV_A * HD, N_KV_A * HD, BRANCH_W,
          N_HEADS_B * HD, N_HEADS_B * HD, N_HEADS_B * HD, BRANCH_W,
          N_HEADS_M * HD_M, BRANCH_W, N_BRANCH * D_MODEL)
_OFF = tuple(int(v) for v in np.cumsum((0,) + _SIZES))
D_IN = _OFF[-1]

VMEM_LIMIT = 56 * 1024 * 1024

PROJ_TM = 256
FLASH_TQ = 256
FLASH_TK = 1024
FLASH_COLS = 256
FLASH_SUB = 256
NBR_QROWS = 2
NBR_KROWS = 10
NBR_BLOCKS_PER_STEP = 64
EPI_TM = 512


def _sigmoid(z):
    return 1.0 / (1.0 + jnp.exp(-z))


def _norm_rope_t(h_t, g_col, cos_t, sin_t):
    ms = jnp.mean(h_t * h_t, axis=0, keepdims=True)
    y = h_t * lax.rsqrt(ms + EPS_QK) * g_col
    parts = []
    for ax in range(2):
        x1 = y[32 * ax:32 * ax + 16]
        x2 = y[32 * ax + 16:32 * ax + 32]
        c = cos_t[16 * ax:16 * ax + 16]
        s = sin_t[16 * ax:16 * ax + 16]
        parts.append(x1 * c - x2 * s)
        parts.append(x2 * c + x1 * s)
    return jnp.concatenate(parts, axis=0)


def _proj_kernel(x_ref, w_ref, cos_ref, sin_ref, gq_ref, gk_ref,
                 qat_ref, ka_ref, vat_ref, sa_ref, qbt_ref, kb_ref, vbt_ref, sb_ref,
                 qm_ref, sm_ref, sg_ref):
    xb = x_ref[...].astype(BF16)

    def seg(j):
        return jnp.dot(xb, w_ref[:, _OFF[j]:_OFF[j + 1]], preferred_element_type=F32)

    cos_t = cos_ref[...]
    sin_t = sin_ref[...]
    scale = HD ** -0.5

    qa_t = seg(0).T
    gq = gq_ref[...]
    heads = [_norm_rope_t(qa_t[HD * h:HD * (h + 1)], gq, cos_t, sin_t) * (scale * LOG2_E)
             for h in range(N_HEADS_A)]
    qat_ref[...] = jnp.concatenate(heads, axis=0).astype(BF16)

    ka_t = seg(1).T
    gk = gk_ref[...]
    kheads = [_norm_rope_t(ka_t[HD * h:HD * (h + 1)], gk, cos_t, sin_t) for h in range(N_KV_A)]
    ka_ref[...] = jnp.concatenate(kheads, axis=0).T.astype(BF16)

    vat_ref[...] = seg(2).T.astype(BF16)
    za = seg(3)
    sa_ref[...] = (za * _sigmoid(za)).astype(BF16)
    qbt_ref[...] = (seg(4) * (scale * LOG2_E)).T.astype(BF16)
    kb_ref[...] = seg(5).astype(BF16)
    vbt_ref[...] = seg(6).T.astype(BF16)
    zb = seg(7)
    sb_ref[...] = (zb * _sigmoid(zb)).astype(BF16)
    qm_ref[...] = seg(8).astype(BF16)
    zm = seg(9)
    sm_ref[...] = (zm * _sigmoid(zm)).astype(BF16)
    sg_ref[...] = _sigmoid(seg(10)).astype(BF16)


def _projection(x, w_in, cos_t, sin_t, gq, gk):
    b, s, d = x.shape
    tm = PROJ_TM
    assert s % tm == 0
    tok = lambda w: pl.BlockSpec((None, tm, w), lambda bi, i: (bi, i, 0))
    chan = lambda w: pl.BlockSpec((None, w, tm), lambda bi, i: (bi, 0, i))
    const = lambda shape: pl.BlockSpec(shape, lambda bi, i: (0,) * len(shape))
    nat = lambda w: jax.ShapeDtypeStruct((b, s, w), BF16)
    tr = lambda w: jax.ShapeDtypeStruct((b, w, s), BF16)
    return pl.pallas_call(
        _proj_kernel,
        grid=(b, s // tm),
        in_specs=[tok(d),
                  pl.BlockSpec((d, D_IN), lambda bi, i: (0, 0), pipeline_mode=pl.Buffered(1)),
                  pl.BlockSpec((2 * ROPE_FREQS, tm), lambda bi, i: (0, i)),
                  pl.BlockSpec((2 * ROPE_FREQS, tm), lambda bi, i: (0, i)),
                  const((HD, 1)), const((HD, 1))],
        out_specs=[chan(512), tok(128), chan(128), tok(512), chan(512), tok(512), chan(512), tok(512),
                   tok(512), tok(512), tok(3072)],
        out_shape=[tr(512), nat(128), tr(128), nat(512), tr(512), nat(512), tr(512), nat(512),
                   nat(512), nat(512), nat(3072)],
        compiler_params=pltpu.CompilerParams(
            dimension_semantics=("parallel", "parallel"), vmem_limit_bytes=VMEM_LIMIT),
        name="proj",
    )(x, w_in, cos_t, sin_t, gq, gk)


def _flash_kernel(qt_ref, k_ref, vt_ref, o_ref, *scratch, tk, nk, stabilize):
    kv = pl.program_id(1)
    tq = qt_ref.shape[1]
    ncol = GROUP_A * tq
    q4 = qt_ref[...].astype(F32)
    qcols = jnp.concatenate([q4[HD * g:HD * (g + 1)] for g in range(GROUP_A)], axis=1)
    q2 = jnp.concatenate([qcols, qcols], axis=0)
    row = lax.broadcasted_iota(jnp.int32, (N_KV_A * HD, ncol), 0)
    q_ext = jnp.where((row // HD) == kv, q2, 0.0).astype(BF16)

    def weighted_values(t, slot, acc):
        k0 = pl.multiple_of(t * tk, tk)
        return acc + jnp.dot(vt_ref[:, pl.ds(k0, tk)], p_bufs[slot][...], preferred_element_type=F32)

    def finish(acc, l):
        o = acc / l
        stacked = jnp.concatenate([o[:, tq * g:tq * (g + 1)] for g in range(GROUP_A)], axis=0)
        o_ref[...] = stacked.T.astype(BF16)

    if not stabilize:
        p_bufs = scratch

        sub = FLASH_SUB
        n_sub = tk // sub

        def step(t_new, slot_new, t_old, slot_old, l, acc):
            cw = FLASH_COLS
            l_parts = [l[:, cw * j:cw * (j + 1)] for j in range(ncol // cw)]
            acc_parts = [acc[:, cw * j:cw * (j + 1)] for j in range(ncol // cw)]
            for c in range(n_sub):
                rows = slice(c * sub, (c + 1) * sub)
                for j in range(ncol // cw):
                    cols = slice(cw * j, cw * (j + 1))
                    if t_new is not None:
                        k0 = pl.multiple_of(t_new * tk + c * sub, sub)
                        p = jnp.exp2(jnp.dot(k_ref[pl.ds(k0, sub), :], q_ext[:, cols],
                                             preferred_element_type=F32))
                        p_bufs[slot_new][rows, cols] = p.astype(BF16)
                        l_parts[j] = l_parts[j] + jnp.sum(p, axis=0, keepdims=True)
                    if t_old is not None:
                        v0 = pl.multiple_of(t_old * tk + c * sub, sub)
                        acc_parts[j] = acc_parts[j] + jnp.dot(
                            vt_ref[:, pl.ds(v0, sub)], p_bufs[slot_old][rows, cols], preferred_element_type=F32)
            return jnp.concatenate(l_parts, axis=1), jnp.concatenate(acc_parts, axis=1)

        l = jnp.zeros((1, ncol), F32)
        acc = jnp.zeros((HD, ncol), F32)
        l, acc = step(0, 0, None, None, l, acc)

        def fast_body(j, carry):
            l, acc = carry
            t = 2 * j + 1
            l, acc = step(t, 1, t - 1, 0, l, acc)
            l, acc = step(t + 1, 0, t, 1, l, acc)
            return l, acc

        l, acc = lax.fori_loop(0, (nk - 2) // 2, fast_body, (l, acc))
        l, acc = step(nk - 1, 1, nk - 2, 0, l, acc)
        l, acc = step(None, None, nk - 1, 1, l, acc)
        finish(acc, l)
        return

    s_bufs = scratch[:2]
    p_bufs = scratch[2:]

    def scores(t, slot):
        k0 = pl.multiple_of(t * tk, tk)
        s = jnp.dot(k_ref[pl.ds(k0, tk), :], q_ext, preferred_element_type=F32)
        s_bufs[slot][...] = s
        return jnp.max(s, axis=0, keepdims=True)

    def softmax(slot, m, l, m_tile):
        m_new = jnp.maximum(m, m_tile)
        alpha = jnp.exp2(m - m_new)
        p = jnp.exp2(s_bufs[slot][...] - m_new)
        p_bufs[slot][...] = p.astype(BF16)
        return m_new, alpha * l + jnp.sum(p, axis=0, keepdims=True), alpha

    m = jnp.full((1, ncol), NEG_INF, F32)
    l = jnp.zeros((1, ncol), F32)
    acc = jnp.zeros((HD, ncol), F32)

    m_tile = scores(0, 0)
    m_next = scores(1, 1)
    m, l, alpha = softmax(0, m, l, m_tile)

    def body(j, carry):
        m, l, acc, alpha_prev, m_tile = carry
        t = 2 * j + 1
        m_next = scores(t + 1, 0)
        m, l, alpha = softmax(1, m, l, m_tile)
        acc = weighted_values(t - 1, 0, alpha_prev * acc)
        m_tile = scores(t + 2, 1)
        m, l, alpha_prev = softmax(0, m, l, m_next)
        acc = weighted_values(t, 1, alpha * acc)
        return m, l, acc, alpha_prev, m_tile

    m, l, acc, alpha_prev, m_tile = lax.fori_loop(0, (nk - 2) // 2, body, (m, l, acc, alpha, m_next))
    m, l, alpha = softmax(1, m, l, m_tile)
    acc = weighted_values(nk - 2, 0, alpha_prev * acc)
    acc = weighted_values(nk - 1, 1, alpha * acc)
    finish(acc, l)


def _flash_call(qa_t, ka, va_t, *, stabilize):
    b, _, s = qa_t.shape
    tq, tk = FLASH_TQ, FLASH_TK
    assert s % tq == 0 and s % (2 * tk) == 0 and s // tk >= 4
    gw = GROUP_A * HD
    ncol = GROUP_A * tq
    scratch = [pltpu.VMEM((tk, ncol), BF16), pltpu.VMEM((tk, ncol), BF16)]
    if stabilize:
        scratch = [pltpu.VMEM((tk, ncol), F32), pltpu.VMEM((tk, ncol), F32)] + scratch
    return pl.pallas_call(
        functools.partial(_flash_kernel, tk=tk, nk=s // tk, stabilize=stabilize),
        grid=(b, N_KV_A, s // tq),
        in_specs=[pl.BlockSpec((None, gw, tq), lambda bi, kv, j: (bi, kv, j)),
                  pl.BlockSpec((None, s, N_KV_A * HD), lambda bi, kv, j: (bi, 0, 0)),
                  pl.BlockSpec((None, HD, s), lambda bi, kv, j: (bi, kv, 0))],
        out_specs=pl.BlockSpec((None, tq, gw), lambda bi, kv, j: (bi, j, kv)),
        out_shape=jax.ShapeDtypeStruct((b, s, N_HEADS_A * HD), BF16),
        scratch_shapes=scratch,
        compiler_params=pltpu.CompilerParams(
            dimension_semantics=("parallel", "parallel", "parallel"), vmem_limit_bytes=VMEM_LIMIT),
        name="flash_stable" if stabilize else "flash",
    )(qa_t, ka, va_t)


def _global_attention(qa_t, ka, va_t, score_bound):
    return lax.cond(score_bound <= FLASH_UNSHIFTED_MAX_SCORE,
                    functools.partial(_flash_call, stabilize=False),
                    functools.partial(_flash_call, stabilize=True),
                    qa_t, ka, va_t)


def _nbr_plan(rows):
    qr, kr = NBR_QROWS, NBR_KROWS
    assert rows % qr == 0 and rows >= kr and rows >= WIN_R
    nb = rows // qr
    nq, nkeys = qr * GRID_W, kr * GRID_W
    q_loc, k_loc = np.arange(nq), np.arange(nkeys)
    q_col, k_col = q_loc % GRID_W, k_loc % GRID_W
    q_wc = np.clip(q_col - WIN_C // 2, 0, GRID_W - WIN_C)
    col_ok = (k_col[:, None] >= q_wc[None]) & (k_col[:, None] < q_wc[None] + WIN_C)
    patterns, blk_k0, blk_pat = {}, [], []
    masks, drs = [], []
    for i in range(nb):
        r0 = i * qr
        rs = min(max(r0 - WIN_R // 2, 0), rows - kr)
        q_row = r0 + q_loc // GRID_W
        k_row = rs + k_loc // GRID_W
        q_wr = np.clip(q_row - WIN_R // 2, 0, rows - WIN_R)
        assert rs <= q_wr.min() and q_wr.max() + WIN_R <= rs + kr
        key = (rs - r0,) + tuple(int(v) for v in (q_wr - r0))
        if key not in patterns:
            patterns[key] = len(patterns)
            row_ok = (k_row[:, None] >= q_wr[None]) & (k_row[:, None] < q_wr[None] + WIN_R)
            masks.append(row_ok & col_ok)
            drs.append(np.clip((rs + np.arange(kr))[:, None] - (r0 + np.arange(qr))[None] + WIN_R - 1,
                               0, 2 * WIN_R - 2))
        blk_k0.append(rs * GRID_W)
        blk_pat.append(patterns[key])
    dc_grid = np.clip(np.arange(GRID_W)[:, None] - np.arange(GRID_W)[None] + WIN_C - 1, 0, 2 * WIN_C - 2)
    return (np.asarray(blk_k0, np.int32), np.asarray(blk_pat, np.int32),
            np.stack(masks), np.stack(drs), dc_grid)


def _nbr_tables(rpb, rows):
    blk_k0, blk_pat, masks, drs, dc_grid = _nbr_plan(rows)
    npat, nkeys, nq = masks.shape
    kr, qr = NBR_KROWS, NBR_QROWS
    row_sel = np.eye(2 * WIN_R - 1, dtype=np.float32)[drs.reshape(-1)]
    col_sel = np.eye(2 * WIN_C - 1, dtype=np.float32)[dc_grid.reshape(-1)].T
    hi = lax.Precision.HIGHEST
    by_row = jnp.einsum('pr,hrc->hpc', row_sel, rpb.astype(F32), precision=hi)
    by_row = by_row.reshape(N_HEADS_B // 2, 2, npat, kr, qr, 2 * WIN_C - 1)
    col_sel = col_sel.reshape(2 * WIN_C - 1, GRID_W, GRID_W)
    full = jnp.einsum('abPKQc,ckq->PaKkbQq', by_row, col_sel, precision=hi)
    full = full.reshape(npat, N_HEADS_B // 2, nkeys, 2 * nq)
    mask2 = np.tile(masks, (1, 1, 2))
    bias = jnp.where(jnp.asarray(mask2)[:, None], full * LOG2_E, NEG_INF)
    return jnp.asarray(blk_k0), jnp.asarray(blk_pat), bias


def _nbr_kernel(k0_ref, pat_ref, qt_ref, k_ref, vt_ref, bias_ref, o_ref,
                s_buf0, s_buf1, p_buf0, p_buf1, *, nkeys, nq):
    i = pl.program_id(2)
    n_blk = qt_ref.shape[1] // nq
    s_bufs = (s_buf0, s_buf1)
    p_bufs = (p_buf0, p_buf1)
    row = lax.broadcasted_iota(jnp.int32, (2 * HD, 2 * nq), 0)
    col = lax.broadcasted_iota(jnp.int32, (2 * HD, 2 * nq), 1)
    pair_diag = (row // HD) == (col // nq)

    def key_start(c):
        return pl.multiple_of(k0_ref[i * n_blk + c], 128)

    def scores(c, slot):
        pat = pat_ref[i * n_blk + c]
        q0 = pl.multiple_of(c * nq, nq)
        q2 = qt_ref[:, pl.ds(q0, nq)].astype(F32)
        qq = jnp.concatenate([q2, q2], axis=1)
        q_ext = jnp.where(pair_diag, qq, 0.0).astype(BF16)
        s = jnp.dot(k_ref[pl.ds(key_start(c), nkeys), :], q_ext, preferred_element_type=F32)
        sc = s + bias_ref[pat]
        s_bufs[slot][...] = sc
        return jnp.max(sc, axis=0, keepdims=True)

    def softmax(slot, m):
        p = jnp.exp2(s_bufs[slot][...] - m)
        p_bufs[slot][...] = p.astype(BF16)
        return jnp.sum(p, axis=0, keepdims=True)

    def output(c, slot, l):
        ot = jnp.dot(vt_ref[:, pl.ds(key_start(c), nkeys)], p_bufs[slot][...],
                     preferred_element_type=F32)
        ot = ot / l
        o2 = jnp.concatenate([ot[0:HD, 0:nq], ot[HD:2 * HD, nq:2 * nq]], axis=0)
        q0 = pl.multiple_of(c * nq, nq)
        o_ref[pl.ds(q0, nq), :] = o2.T.astype(BF16)

    m0 = scores(0, 0)
    m1 = scores(1, 1)
    l0 = softmax(0, m0)
    m0 = scores(2, 0)
    l1 = softmax(1, m1)
    m1 = scores(3, 1)

    def body(j, carry):
        m0, m1, l0, l1 = carry
        c = 2 * j
        output(c - 2, 0, l0)
        l0 = softmax(0, m0)
        m0 = scores(c + 2, 0)
        output(c - 1, 1, l1)
        l1 = softmax(1, m1)
        m1 = scores(c + 3, 1)
        return m0, m1, l0, l1

    m0, m1, l0, l1 = lax.fori_loop(1, n_blk // 2 - 1, body, (m0, m1, l0, l1))
    output(n_blk - 4, 0, l0)
    l0 = softmax(0, m0)
    output(n_blk - 3, 1, l1)
    l1 = softmax(1, m1)
    output(n_blk - 2, 0, l0)
    output(n_blk - 1, 1, l1)


def _nbr_attention(qb_t, kb, vb_t, tables):
    blk_k0, blk_pat, bias = tables
    b, _, s = qb_t.shape
    nq, nkeys = NBR_QROWS * GRID_W, NBR_KROWS * GRID_W
    npat = bias.shape[0]
    hp = N_HEADS_B // 2
    n_blk = min(NBR_BLOCKS_PER_STEP, s // nq)
    tq = nq * n_blk
    assert s % tq == 0 and n_blk % 2 == 0 and n_blk >= 4
    grid_spec = pltpu.PrefetchScalarGridSpec(
        num_scalar_prefetch=2,
        grid=(hp, b, s // tq),
        in_specs=[pl.BlockSpec((None, 2 * HD, tq), lambda h, bi, i, *_: (bi, h, i)),
                  pl.BlockSpec((None, s, 2 * HD), lambda h, bi, i, *_: (bi, 0, h)),
                  pl.BlockSpec((None, 2 * HD, s), lambda h, bi, i, *_: (bi, h, 0)),
                  pl.BlockSpec((npat, None, nkeys, 2 * nq), lambda h, bi, i, *_: (0, h, 0, 0))],
        out_specs=pl.BlockSpec((None, tq, 2 * HD), lambda h, bi, i, *_: (bi, i, h)),
        scratch_shapes=[pltpu.VMEM((nkeys, 2 * nq), F32), pltpu.VMEM((nkeys, 2 * nq), F32),
                        pltpu.VMEM((nkeys, 2 * nq), BF16), pltpu.VMEM((nkeys, 2 * nq), BF16)],
    )
    return pl.pallas_call(
        functools.partial(_nbr_kernel, nkeys=nkeys, nq=nq),
        grid_spec=grid_spec,
        out_shape=jax.ShapeDtypeStruct((b, s, N_HEADS_B * HD), BF16),
        compiler_params=pltpu.CompilerParams(
            dimension_semantics=("parallel", "parallel", "parallel"), vmem_limit_bytes=VMEM_LIMIT),
        name="nbr",
    )(blk_k0, blk_pat, qb_t, kb, vb_t, bias)


def _memkv_kernel(mem_ref, w_ref, kmt_ref, vm_ref):
    kv = jnp.dot(mem_ref[...].astype(BF16), w_ref[...], preferred_element_type=F32)
    half = N_HEADS_M * HD_M
    kmt_ref[...] = kv[:, :half].T.astype(BF16)
    vm_ref[...] = kv[:, half:].astype(BF16)


def _memory_kv(mem, w_mem_kv):
    b, n, d = mem.shape
    half = N_HEADS_M * HD_M
    return pl.pallas_call(
        _memkv_kernel,
        grid=(b,),
        in_specs=[pl.BlockSpec((None, n, d), lambda bi: (bi, 0, 0)),
                  pl.BlockSpec((d, 2 * half), lambda bi: (0, 0))],
        out_specs=[pl.BlockSpec((None, half, n), lambda bi: (bi, 0, 0)),
                   pl.BlockSpec((None, n, half), lambda bi: (bi, 0, 0))],
        out_shape=[jax.ShapeDtypeStruct((b, half, n), BF16), jax.ShapeDtypeStruct((b, n, half), BF16)],
        compiler_params=pltpu.CompilerParams(
            dimension_semantics=("parallel",), vmem_limit_bytes=VMEM_LIMIT),
        name="memkv",
    )(mem, w_mem_kv)


def _epilogue_kernel(x_ref, oa_ref, ob_ref, sa_ref, sb_ref, sm_ref, qm_ref, sg_ref, kmt_ref, vm_ref,
                     wb_ref, wo_ref, g_ref, b_ref, y_ref, *, alpha):
    mscale = HD_M ** -0.5
    heads = []
    for h in range(N_HEADS_M):
        lo, hi = HD_M * h, HD_M * (h + 1)
        s = jnp.dot(qm_ref[:, lo:hi], kmt_ref[lo:hi, :], preferred_element_type=F32) * mscale
        m = jnp.max(s, axis=-1, keepdims=True)
        p = jnp.exp(s - m)
        l = jnp.sum(p, axis=-1, keepdims=True)
        heads.append(jnp.dot(p.astype(BF16), vm_ref[:, lo:hi], preferred_element_type=F32) / l)
    om = jnp.concatenate(heads, axis=-1)

    branches = (oa_ref[...].astype(F32) * sa_ref[...].astype(F32),
                ob_ref[...].astype(F32) * sb_ref[...].astype(F32),
                om * sm_ref[...].astype(F32))
    merged = None
    for n, g in enumerate(branches):
        proj = jnp.dot(g.astype(BF16), wb_ref[n], preferred_element_type=F32)
        term = sg_ref[:, D_MODEL * n:D_MODEL * (n + 1)].astype(F32) * proj
        merged = term if merged is None else merged + term
    out = jnp.dot(merged.astype(BF16), wo_ref[...], preferred_element_type=F32)

    r = alpha * x_ref[...] + out
    mu = jnp.mean(r, axis=-1, keepdims=True)
    c = r - mu
    var = jnp.mean(c * c, axis=-1, keepdims=True)
    y_ref[...] = c * lax.rsqrt(var + EPS_LN) * g_ref[...] + b_ref[...]


def _epilogue(x, oa, ob, sa, sb, sm, qm, sg, km_t, vm, w_branch, w_out, ln_g, ln_b, alpha):
    b, s, d = x.shape
    tm = EPI_TM
    assert s % tm == 0
    n_mem = vm.shape[1]
    half = N_HEADS_M * HD_M
    tok = lambda w: pl.BlockSpec((None, tm, w), lambda bi, i: (bi, i, 0))
    const = lambda shape: pl.BlockSpec(shape, lambda bi, i: (0,) * len(shape))
    return pl.pallas_call(
        functools.partial(_epilogue_kernel, alpha=alpha),
        grid=(b, s // tm),
        in_specs=[tok(d), tok(512), tok(512), tok(512), tok(512), tok(512), tok(512), tok(3072),
                  pl.BlockSpec((None, half, n_mem), lambda bi, i: (bi, 0, 0)),
                  pl.BlockSpec((None, n_mem, half), lambda bi, i: (bi, 0, 0)),
                  const((N_BRANCH, BRANCH_W, d)), const((d, d)), const((1, d)), const((1, d))],
        out_specs=tok(d),
        out_shape=jax.ShapeDtypeStruct((b, s, d), F32),
        compiler_params=pltpu.CompilerParams(
            dimension_semantics=("parallel", "parallel"), vmem_limit_bytes=VMEM_LIMIT),
        name="epilogue",
    )(x, oa, ob, sa, sb, sm, qm, sg, km_t, vm, w_branch, w_out, ln_g, ln_b)


def _rope_tables_t(n_tok):
    t = jnp.arange(n_tok)
    pos = jnp.stack([t // GRID_W, t % GRID_W], axis=0).astype(F32)
    inv_freq = ROPE_THETA ** (-jnp.arange(ROPE_FREQS, dtype=F32) / ROPE_FREQS)
    ang = (pos[:, None, :] * inv_freq[None, :, None]).reshape(2 * ROPE_FREQS, n_tok)
    return jnp.cos(ang), jnp.sin(ang)


def _layer(x, mem, w_in, q_norm, k_norm, rpb, w_mem_kv, w_branch, w_out, ln_g, ln_b, alpha):
    _, s, _ = x.shape
    assert s % GRID_W == 0
    cos_t, sin_t = _rope_tables_t(s)
    (qa_t, ka, va_t, sa, qb_t, kb, vb_t, sb, qm, sm, sg) = _projection(
        x, w_in.astype(BF16), cos_t, sin_t, q_norm.reshape(HD, 1), k_norm.reshape(HD, 1))
    score_bound = (LOG2_E * HD ** 0.5 * BF16_ROUNDING_SLACK) * jnp.max(jnp.abs(q_norm)) * jnp.max(jnp.abs(k_norm))
    oa = _global_attention(qa_t, ka, va_t, score_bound)
    ob = _nbr_attention(qb_t, kb, vb_t, _nbr_tables(rpb, s // GRID_W))
    km_t, vm = _memory_kv(mem, w_mem_kv.astype(BF16))
    return _epilogue(x, oa, ob, sa, sb, sm, qm, sg, km_t, vm, w_branch.astype(BF16), w_out.astype(BF16),
                     ln_g.reshape(1, D_MODEL), ln_b.reshape(1, D_MODEL), alpha)


def kernel(x_prompt, x_sample, mem_prompt, mem_sample, w_in, q_norm, k_norm, rpb, w_mem_kv, w_branch, w_out,
           ln_g, ln_b):
    depth = w_in.shape[0]
    alpha = (2 * depth) ** 0.25
    y_prompt, y_sample = x_prompt, x_sample
    for l in range(depth):
        args = (w_in[l], q_norm[l], k_norm[l], rpb[l], w_mem_kv[l], w_branch[l], w_out[l], ln_g[l], ln_b[l])
        y_prompt = _layer(y_prompt, mem_prompt, *args, alpha)
        y_sample = _layer(y_sample, mem_sample, *args, alpha)
    return (y_prompt, y_sample)
```

```python
import functools

import numpy as np
import jax
import jax.numpy as jnp
from jax import lax
from jax.experimental import pallas as pl
from jax.experimental.pallas import tpu as pltpu

F32 = jnp.float32
BF16 = jnp.bfloat16

D_MODEL = 1024
GRID_W = 64
HD = 64
N_HEADS_A = 8
N_KV_A = 2
GROUP_A = N_HEADS_A // N_KV_A
N_HEADS_B = 8
N_HEADS_M = 4
HD_M = 128
WIN_R = 8
WIN_C = 16
BRANCH_W = 512
N_BRANCH = 3
ROPE_THETA = 10000.0
ROPE_FREQS = HD // 4
EPS_QK = 1e-6
EPS_LN = 1e-5
NEG_INF = -1e30
LOG2_E = 1.4426950408889634
FLASH_UNSHIFTED_MAX_SCORE = 60.0
BF16_ROUNDING_SLACK = 1.02

_SIZES = (N_HEADS_A * HD, N_KV_A * HD, N_KV_A * HD, BRANCH_W,
          N_HEADS_B * HD, N_HEADS_B * HD, N_HEADS_B * HD, BRANCH_W,
          N_HEADS_M * HD_M, BRANCH_W, N_BRANCH * D_MODEL)
_OFF = tuple(int(v) for v in np.cumsum((0,) + _SIZES))
D_IN = _OFF[-1]

V7X_VMEM_BYTES = 64 * 1024 * 1024
VMEM_COMPILER_RESERVE = 8 * 1024 * 1024
VMEM_LIMIT = V7X_VMEM_BYTES - VMEM_COMPILER_RESERVE

PROJ_TM = 256
FLASH_TQ = 256
FLASH_TK = 1024
FLASH_COLS = 512
FLASH_SUB = 256
NBR_QROWS = 2
NBR_KROWS = 10
NBR_BLOCKS_PER_STEP = 32
EPI_TM = 512


def _sigmoid(z):
    return 1.0 / (1.0 + jnp.exp(-z))


def _norm_rope_t(h_t, g_col, cos_t, sin_t):
    ms = jnp.mean(h_t * h_t, axis=0, keepdims=True)
    y = h_t * lax.rsqrt(ms + EPS_QK) * g_col
    parts = []
    for ax in range(2):
        x1 = y[32 * ax:32 * ax + 16]
        x2 = y[32 * ax + 16:32 * ax + 32]
        c = cos_t[16 * ax:16 * ax + 16]
        s = sin_t[16 * ax:16 * ax + 16]
        parts.append(x1 * c - x2 * s)
        parts.append(x2 * c + x1 * s)
    return jnp.concatenate(parts, axis=0)


def _proj_kernel(x_ref, w_ref, cos_ref, sin_ref, gq_ref, gk_ref,
                 qat_ref, ka_ref, vat_ref, sa_ref, qbt_ref, kb_ref, vbt_ref, sb_ref,
                 qm_ref, sm_ref, sg_ref):
    xb = x_ref[...].astype(BF16)

    def seg(j):
        return jnp.dot(xb, w_ref[:, _OFF[j]:_OFF[j + 1]], preferred_element_type=F32)

    cos_t = cos_ref[...]
    sin_t = sin_ref[...]
    scale = HD ** -0.5

    qa_t = seg(0).T
    gq = gq_ref[...]
    heads = [_norm_rope_t(qa_t[HD * h:HD * (h + 1)], gq, cos_t, sin_t) * (scale * LOG2_E)
             for h in range(N_HEADS_A)]
    qat_ref[...] = jnp.concatenate(heads, axis=0).astype(BF16)

    ka_t = seg(1).T
    gk = gk_ref[...]
    kheads = [_norm_rope_t(ka_t[HD * h:HD * (h + 1)], gk, cos_t, sin_t) for h in range(N_KV_A)]
    ka_ref[...] = jnp.concatenate(kheads, axis=0).T.astype(BF16)

    vat_ref[...] = seg(2).T.astype(BF16)
    za = seg(3)
    sa_ref[...] = (za * _sigmoid(za)).astype(BF16)
    qbt_ref[...] = (seg(4) * (scale * LOG2_E)).T.astype(BF16)
    kb_ref[...] = seg(5).astype(BF16)
    vbt_ref[...] = seg(6).T.astype(BF16)
    zb = seg(7)
    sb_ref[...] = (zb * _sigmoid(zb)).astype(BF16)
    qm_ref[...] = seg(8).astype(BF16)
    zm = seg(9)
    sm_ref[...] = (zm * _sigmoid(zm)).astype(BF16)
    sg_ref[...] = _sigmoid(seg(10)).astype(BF16)


def _projection(x, w_in, cos_t, sin_t, gq, gk):
    b, s, d = x.shape
    tm = PROJ_TM
    assert s % tm == 0
    tok = lambda w: pl.BlockSpec((None, tm, w), lambda bi, i: (bi, i, 0))
    chan = lambda w: pl.BlockSpec((None, w, tm), lambda bi, i: (bi, 0, i))
    const = lambda shape: pl.BlockSpec(shape, lambda bi, i: (0,) * len(shape))
    nat = lambda w: jax.ShapeDtypeStruct((b, s, w), BF16)
    tr = lambda w: jax.ShapeDtypeStruct((b, w, s), BF16)
    return pl.pallas_call(
        _proj_kernel,
        grid=(b, s // tm),
        in_specs=[tok(d),
                  pl.BlockSpec((d, D_IN), lambda bi, i: (0, 0), pipeline_mode=pl.Buffered(1)),
                  pl.BlockSpec((2 * ROPE_FREQS, tm), lambda bi, i: (0, i)),
                  pl.BlockSpec((2 * ROPE_FREQS, tm), lambda bi, i: (0, i)),
                  const((HD, 1)), const((HD, 1))],
        out_specs=[chan(512), tok(128), chan(128), tok(512), chan(512), tok(512), chan(512), tok(512),
                   tok(512), tok(512), tok(3072)],
        out_shape=[tr(512), nat(128), tr(128), nat(512), tr(512), nat(512), tr(512), nat(512),
                   nat(512), nat(512), nat(3072)],
        compiler_params=pltpu.CompilerParams(
            dimension_semantics=("parallel", "parallel"), vmem_limit_bytes=VMEM_LIMIT),
        name="proj",
    )(x, w_in, cos_t, sin_t, gq, gk)


def _flash_kernel(qt_ref, k_ref, vt_ref, o_ref, *scratch, tk, nk, stabilize):
    kv = pl.program_id(1)
    tq = qt_ref.shape[1]
    ncol = GROUP_A * tq
    q4 = qt_ref[...].astype(F32)
    qcols = jnp.concatenate([q4[HD * g:HD * (g + 1)] for g in range(GROUP_A)], axis=1)
    q2 = jnp.concatenate([qcols, qcols], axis=0)
    row = lax.broadcasted_iota(jnp.int32, (N_KV_A * HD, ncol), 0)
    q_ext = jnp.where((row // HD) == kv, q2, 0.0).astype(BF16)

    def weighted_values(t, slot, acc):
        k0 = pl.multiple_of(t * tk, tk)
        return acc + jnp.dot(vt_ref[:, pl.ds(k0, tk)], p_bufs[slot][...], preferred_element_type=F32)

    def finish(acc, l):
        o = acc / l
        stacked = jnp.concatenate([o[:, tq * g:tq * (g + 1)] for g in range(GROUP_A)], axis=0)
        o_ref[...] = stacked.T.astype(BF16)

    if not stabilize:
        p_bufs = scratch

        sub = FLASH_SUB
        n_sub = tk // sub

        def step(t_new, slot_new, t_old, slot_old, l, acc):
            cw = FLASH_COLS
            l_parts = [l[:, cw * j:cw * (j + 1)] for j in range(ncol // cw)]
            acc_parts = [acc[:, cw * j:cw * (j + 1)] for j in range(ncol // cw)]
            for c in range(n_sub):
                rows = slice(c * sub, (c + 1) * sub)
                for j in range(ncol // cw):
                    cols = slice(cw * j, cw * (j + 1))
                    if t_new is not None:
                        k0 = pl.multiple_of(t_new * tk + c * sub, sub)
                        p = jnp.exp2(jnp.dot(k_ref[pl.ds(k0, sub), :], q_ext[:, cols],
                                             preferred_element_type=F32))
                        p_bufs[slot_new][rows, cols] = p.astype(BF16)
                        l_parts[j] = l_parts[j] + jnp.sum(p, axis=0, keepdims=True)
                    if t_old is not None:
                        v0 = pl.multiple_of(t_old * tk + c * sub, sub)
                        acc_parts[j] = acc_parts[j] + jnp.dot(
                            vt_ref[:, pl.ds(v0, sub)], p_bufs[slot_old][rows, cols], preferred_element_type=F32)
            return jnp.concatenate(l_parts, axis=1), jnp.concatenate(acc_parts, axis=1)

        l = jnp.zeros((1, ncol), F32)
        acc = jnp.zeros((HD, ncol), F32)
        l, acc = step(0, 0, None, None, l, acc)

        def fast_body(j, carry):
            l, acc = carry
            t = 2 * j + 1
            l, acc = step(t, 1, t - 1, 0, l, acc)
            l, acc = step(t + 1, 0, t, 1, l, acc)
            return l, acc

        l, acc = lax.fori_loop(0, (nk - 2) // 2, fast_body, (l, acc), unroll=True)
        l, acc = step(nk - 1, 1, nk - 2, 0, l, acc)
        l, acc = step(None, None, nk - 1, 1, l, acc)
        finish(acc, l)
        return

    s_bufs = scratch[:2]
    p_bufs = scratch[2:]

    def scores(t, slot):
        k0 = pl.multiple_of(t * tk, tk)
        s = jnp.dot(k_ref[pl.ds(k0, tk), :], q_ext, preferred_element_type=F32)
        s_bufs[slot][...] = s
        return jnp.max(s, axis=0, keepdims=True)

    def softmax(slot, m, l, m_tile):
        m_new = jnp.maximum(m, m_tile)
        alpha = jnp.exp2(m - m_new)
        p = jnp.exp2(s_bufs[slot][...] - m_new)
        p_bufs[slot][...] = p.astype(BF16)
        return m_new, alpha * l + jnp.sum(p, axis=0, keepdims=True), alpha

    m = jnp.full((1, ncol), NEG_INF, F32)
    l = jnp.zeros((1, ncol), F32)
    acc = jnp.zeros((HD, ncol), F32)

    m_tile = scores(0, 0)
    m_next = scores(1, 1)
    m, l, alpha = softmax(0, m, l, m_tile)

    def body(j, carry):
        m, l, acc, alpha_prev, m_tile = carry
        t = 2 * j + 1
        m_next = scores(t + 1, 0)
        m, l, alpha = softmax(1, m, l, m_tile)
        acc = weighted_values(t - 1, 0, alpha_prev * acc)
        m_tile = scores(t + 2, 1)
        m, l, alpha_prev = softmax(0, m, l, m_next)
        acc = weighted_values(t, 1, alpha * acc)
        return m, l, acc, alpha_prev, m_tile

    m, l, acc, alpha_prev, m_tile = lax.fori_loop(0, (nk - 2) // 2, body, (m, l, acc, alpha, m_next))
    m, l, alpha = softmax(1, m, l, m_tile)
    acc = weighted_values(nk - 2, 0, alpha_prev * acc)
    acc = weighted_values(nk - 1, 1, alpha * acc)
    finish(acc, l)


def _flash_call(qa_t, ka, va_t, *, stabilize):
    b, _, s = qa_t.shape
    tq, tk = FLASH_TQ, FLASH_TK
    assert s % tq == 0 and s % (2 * tk) == 0 and s // tk >= 4
    gw = GROUP_A * HD
    ncol = GROUP_A * tq
    scratch = [pltpu.VMEM((tk, ncol), BF16), pltpu.VMEM((tk, ncol), BF16)]
    if stabilize:
        scratch = [pltpu.VMEM((tk, ncol), F32), pltpu.VMEM((tk, ncol), F32)] + scratch
    return pl.pallas_call(
        functools.partial(_flash_kernel, tk=tk, nk=s // tk, stabilize=stabilize),
        grid=(b, N_KV_A, s // tq),
        in_specs=[pl.BlockSpec((None, gw, tq), lambda bi, kv, j: (bi, kv, j)),
                  pl.BlockSpec((None, s, N_KV_A * HD), lambda bi, kv, j: (bi, 0, 0)),
                  pl.BlockSpec((None, HD, s), lambda bi, kv, j: (bi, kv, 0))],
        out_specs=pl.BlockSpec((None, tq, gw), lambda bi, kv, j: (bi, j, kv)),
        out_shape=jax.ShapeDtypeStruct((b, s, N_HEADS_A * HD), BF16),
        scratch_shapes=scratch,
        compiler_params=pltpu.CompilerParams(
            dimension_semantics=("parallel", "parallel", "parallel"), vmem_limit_bytes=VMEM_LIMIT),
        name="flash_stable" if stabilize else "flash",
    )(qa_t, ka, va_t)


def _global_attention(qa_t, ka, va_t, score_bound):
    return lax.cond(score_bound <= FLASH_UNSHIFTED_MAX_SCORE,
                    functools.partial(_flash_call, stabilize=False),
                    functools.partial(_flash_call, stabilize=True),
                    qa_t, ka, va_t)


def _nbr_plan(rows):
    qr, kr = NBR_QROWS, NBR_KROWS
    assert rows % qr == 0 and rows >= kr and rows >= WIN_R
    nb = rows // qr
    nq, nkeys = qr * GRID_W, kr * GRID_W
    q_loc, k_loc = np.arange(nq), np.arange(nkeys)
    q_col, k_col = q_loc % GRID_W, k_loc % GRID_W
    q_wc = np.clip(q_col - WIN_C // 2, 0, GRID_W - WIN_C)
    col_ok = (k_col[:, None] >= q_wc[None]) & (k_col[:, None] < q_wc[None] + WIN_C)
    patterns, blk_k0, blk_pat = {}, [], []
    masks, drs = [], []
    for i in range(nb):
        r0 = i * qr
        rs = min(max(r0 - WIN_R // 2, 0), rows - kr)
        q_row = r0 + q_loc // GRID_W
        k_row = rs + k_loc // GRID_W
        q_wr = np.clip(q_row - WIN_R // 2, 0, rows - WIN_R)
        assert rs <= q_wr.min() and q_wr.max() + WIN_R <= rs + kr
        key = (rs - r0,) + tuple(int(v) for v in (q_wr - r0))
        if key not in patterns:
            patterns[key] = len(patterns)
            row_ok = (k_row[:, None] >= q_wr[None]) & (k_row[:, None] < q_wr[None] + WIN_R)
            masks.append(row_ok & col_ok)
            drs.append(np.clip((rs + np.arange(kr))[:, None] - (r0 + np.arange(qr))[None] + WIN_R - 1,
                               0, 2 * WIN_R - 2))
        blk_k0.append(rs * GRID_W)
        blk_pat.append(patterns[key])
    dc_grid = np.clip(np.arange(GRID_W)[:, None] - np.arange(GRID_W)[None] + WIN_C - 1, 0, 2 * WIN_C - 2)
    return (np.asarray(blk_k0, np.int32), np.asarray(blk_pat, np.int32),
            np.stack(masks), np.stack(drs), dc_grid)


def _nbr_tables(rpb, rows):
    blk_k0, blk_pat, masks, drs, dc_grid = _nbr_plan(rows)
    npat, nkeys, nq = masks.shape
    kr, qr = NBR_KROWS, NBR_QROWS
    row_sel = np.eye(2 * WIN_R - 1, dtype=np.float32)[drs.reshape(-1)]
    col_sel = np.eye(2 * WIN_C - 1, dtype=np.float32)[dc_grid.reshape(-1)].T
    hi = lax.Precision.HIGHEST
    by_row = jnp.einsum('pr,hrc->hpc', row_sel, rpb.astype(F32), precision=hi)
    full = jnp.einsum('hpc,cx->hpx', by_row, col_sel, precision=hi)
    full = full.reshape(N_HEADS_B // 2, 2, npat, kr, qr, GRID_W, GRID_W).transpose(2, 0, 3, 5, 1, 4, 6)
    full = full.reshape(npat, N_HEADS_B // 2, nkeys, 2 * nq)
    mask2 = np.tile(masks, (1, 1, 2))
    bias = jnp.where(jnp.asarray(mask2)[:, None], full * LOG2_E, NEG_INF)
    return jnp.asarray(blk_k0), jnp.asarray(blk_pat), bias


def _nbr_kernel(k0_ref, pat_ref, qt_ref, k_ref, vt_ref, bias_ref, o_ref,
                s_buf0, s_buf1, p_buf0, p_buf1, *, nkeys, nq):
    i = pl.program_id(2)
    n_blk = qt_ref.shape[1] // nq
    s_bufs = (s_buf0, s_buf1)
    p_bufs = (p_buf0, p_buf1)
    row = lax.broadcasted_iota(jnp.int32, (2 * HD, 2 * nq), 0)
    col = lax.broadcasted_iota(jnp.int32, (2 * HD, 2 * nq), 1)
    pair_diag = (row // HD) == (col // nq)

    def key_start(c):
        return pl.multiple_of(k0_ref[i * n_blk + c], 128)

    def scores(c, slot):
        pat = pat_ref[i * n_blk + c]
        q0 = pl.multiple_of(c * nq, nq)
        q2 = qt_ref[:, pl.ds(q0, nq)].astype(F32)
        qq = jnp.concatenate([q2, q2], axis=1)
        q_ext = jnp.where(pair_diag, qq, 0.0).astype(BF16)
        s = jnp.dot(k_ref[pl.ds(key_start(c), nkeys), :], q_ext, preferred_element_type=F32)
        sc = s + bias_ref[pat]
        s_bufs[slot][...] = sc
        return jnp.max(sc, axis=0, keepdims=True)

    def softmax(slot, m):
        p = jnp.exp2(s_bufs[slot][...] - m)
        p_bufs[slot][...] = p.astype(BF16)
        return jnp.sum(p, axis=0, keepdims=True)

    def output(c, slot, l):
        ot = jnp.dot(vt_ref[:, pl.ds(key_start(c), nkeys)], p_bufs[slot][...],
                     preferred_element_type=F32)
        ot = ot / l
        o2 = jnp.concatenate([ot[0:HD, 0:nq], ot[HD:2 * HD, nq:2 * nq]], axis=0)
        q0 = pl.multiple_of(c * nq, nq)
        o_ref[pl.ds(q0, nq), :] = o2.T.astype(BF16)

    m0 = scores(0, 0)
    m1 = scores(1, 1)
    l0 = softmax(0, m0)
    m0 = scores(2, 0)
    l1 = softmax(1, m1)
    m1 = scores(3, 1)

    def body(j, carry):
        m0, m1, l0, l1 = carry
        c = 2 * j
        output(c - 2, 0, l0)
        l0 = softmax(0, m0)
        m0 = scores(c + 2, 0)
        output(c - 1, 1, l1)
        l1 = softmax(1, m1)
        m1 = scores(c + 3, 1)
        return m0, m1, l0, l1

    m0, m1, l0, l1 = lax.fori_loop(1, n_blk // 2 - 1, body, (m0, m1, l0, l1))
    output(n_blk - 4, 0, l0)
    l0 = softmax(0, m0)
    output(n_blk - 3, 1, l1)
    l1 = softmax(1, m1)
    output(n_blk - 2, 0, l0)
    output(n_blk - 1, 1, l1)


def _nbr_attention(qb_t, kb, vb_t, tables):
    blk_k0, blk_pat, bias = tables
    b, _, s = qb_t.shape
    nq, nkeys = NBR_QROWS * GRID_W, NBR_KROWS * GRID_W
    npat = bias.shape[0]
    hp = N_HEADS_B // 2
    n_blk = min(NBR_BLOCKS_PER_STEP, s // nq)
    tq = nq * n_blk
    assert s % tq == 0 and n_blk % 2 == 0 and n_blk >= 4
    grid_spec = pltpu.PrefetchScalarGridSpec(
        num_scalar_prefetch=2,
        grid=(hp, b, s // tq),
        in_specs=[pl.BlockSpec((None, 2 * HD, tq), lambda h, bi, i, *_: (bi, h, i)),
                  pl.BlockSpec((None, s, 2 * HD), lambda h, bi, i, *_: (bi, 0, h)),
                  pl.BlockSpec((None, 2 * HD, s), lambda h, bi, i, *_: (bi, h, 0)),
                  pl.BlockSpec((npat, None, nkeys, 2 * nq), lambda h, bi, i, *_: (0, h, 0, 0))],
        out_specs=pl.BlockSpec((None, tq, 2 * HD), lambda h, bi, i, *_: (bi, i, h)),
        scratch_shapes=[pltpu.VMEM((nkeys, 2 * nq), F32), pltpu.VMEM((nkeys, 2 * nq), F32),
                        pltpu.VMEM((nkeys, 2 * nq), BF16), pltpu.VMEM((nkeys, 2 * nq), BF16)],
    )
    return pl.pallas_call(
        functools.partial(_nbr_kernel, nkeys=nkeys, nq=nq),
        grid_spec=grid_spec,
        out_shape=jax.ShapeDtypeStruct((b, s, N_HEADS_B * HD), BF16),
        compiler_params=pltpu.CompilerParams(
            dimension_semantics=("parallel", "parallel", "parallel"), vmem_limit_bytes=VMEM_LIMIT),
        name="nbr",
    )(blk_k0, blk_pat, qb_t, kb, vb_t, bias)


def _memkv_kernel(mem_ref, w_ref, kmt_ref, vm_ref):
    kv = jnp.dot(mem_ref[...].astype(BF16), w_ref[...], preferred_element_type=F32)
    half = N_HEADS_M * HD_M
    kmt_ref[...] = kv[:, :half].T.astype(BF16)
    vm_ref[...] = kv[:, half:].astype(BF16)


def _memory_kv(mem, w_mem_kv):
    b, n, d = mem.shape
    half = N_HEADS_M * HD_M
    return pl.pallas_call(
        _memkv_kernel,
        grid=(b,),
        in_specs=[pl.BlockSpec((None, n, d), lambda bi: (bi, 0, 0)),
                  pl.BlockSpec((d, 2 * half), lambda bi: (0, 0))],
        out_specs=[pl.BlockSpec((None, half, n), lambda bi: (bi, 0, 0)),
                   pl.BlockSpec((None, n, half), lambda bi: (bi, 0, 0))],
        out_shape=[jax.ShapeDtypeStruct((b, half, n), BF16), jax.ShapeDtypeStruct((b, n, half), BF16)],
        compiler_params=pltpu.CompilerParams(
            dimension_semantics=("parallel",), vmem_limit_bytes=VMEM_LIMIT),
        name="memkv",
    )(mem, w_mem_kv)


def _epilogue_kernel(x_ref, oa_ref, ob_ref, sa_ref, sb_ref, sm_ref, qm_ref, sg_ref, kmt_ref, vm_ref,
                     wb_ref, wo_ref, g_ref, b_ref, y_ref, *, alpha):
    mscale = HD_M ** -0.5
    heads = []
    for h in range(N_HEADS_M):
        lo, hi = HD_M * h, HD_M * (h + 1)
        s = jnp.dot(qm_ref[:, lo:hi], kmt_ref[lo:hi, :], preferred_element_type=F32) * mscale
        m = jnp.max(s, axis=-1, keepdims=True)
        p = jnp.exp(s - m)
        l = jnp.sum(p, axis=-1, keepdims=True)
        heads.append(jnp.dot(p.astype(BF16), vm_ref[:, lo:hi], preferred_element_type=F32) / l)
    om = jnp.concatenate(heads, axis=-1)

    branches = (oa_ref[...].astype(F32) * sa_ref[...].astype(F32),
                ob_ref[...].astype(F32) * sb_ref[...].astype(F32),
                om * sm_ref[...].astype(F32))
    merged = None
    for n, g in enumerate(branches):
        proj = jnp.dot(g.astype(BF16), wb_ref[n], preferred_element_type=F32)
        term = sg_ref[:, D_MODEL * n:D_MODEL * (n + 1)].astype(F32) * proj
        merged = term if merged is None else merged + term
    out = jnp.dot(merged.astype(BF16), wo_ref[...], preferred_element_type=F32)

    r = alpha * x_ref[...] + out
    mu = jnp.mean(r, axis=-1, keepdims=True)
    c = r - mu
    var = jnp.mean(c * c, axis=-1, keepdims=True)
    y_ref[...] = c * lax.rsqrt(var + EPS_LN) * g_ref[...] + b_ref[...]


def _epilogue(x, oa, ob, sa, sb, sm, qm, sg, km_t, vm, w_branch, w_out, ln_g, ln_b, alpha):
    b, s, d = x.shape
    tm = EPI_TM
    assert s % tm == 0
    n_mem = vm.shape[1]
    half = N_HEADS_M * HD_M
    tok = lambda w: pl.BlockSpec((None, tm, w), lambda bi, i: (bi, i, 0))
    const = lambda shape: pl.BlockSpec(shape, lambda bi, i: (0,) * len(shape))
    return pl.pallas_call(
        functools.partial(_epilogue_kernel, alpha=alpha),
        grid=(b, s // tm),
        in_specs=[tok(d), tok(512), tok(512), tok(512), tok(512), tok(512), tok(512), tok(3072),
                  pl.BlockSpec((None, half, n_mem), lambda bi, i: (bi, 0, 0)),
                  pl.BlockSpec((None, n_mem, half), lambda bi, i: (bi, 0, 0)),
                  const((N_BRANCH, BRANCH_W, d)), const((d, d)), const((1, d)), const((1, d))],
        out_specs=tok(d),
        out_shape=jax.ShapeDtypeStruct((b, s, d), F32),
        compiler_params=pltpu.CompilerParams(
            dimension_semantics=("parallel", "parallel"), vmem_limit_bytes=VMEM_LIMIT),
        name="epilogue",
    )(x, oa, ob, sa, sb, sm, qm, sg, km_t, vm, w_branch, w_out, ln_g, ln_b)


def _rope_tables_t(n_tok):
    t = jnp.arange(n_tok)
    pos = jnp.stack([t // GRID_W, t % GRID_W], axis=0).astype(F32)
    inv_freq = ROPE_THETA ** (-jnp.arange(ROPE_FREQS, dtype=F32) / ROPE_FREQS)
    ang = (pos[:, None, :] * inv_freq[None, :, None]).reshape(2 * ROPE_FREQS, n_tok)
    return jnp.cos(ang), jnp.sin(ang)


def _layer(x, mem, w_in, q_norm, k_norm, rpb, w_mem_kv, w_branch, w_out, ln_g, ln_b, alpha):
    _, s, _ = x.shape
    assert s % GRID_W == 0
    cos_t, sin_t = _rope_tables_t(s)
    (qa_t, ka, va_t, sa, qb_t, kb, vb_t, sb, qm, sm, sg) = _projection(
        x, w_in.astype(BF16), cos_t, sin_t, q_norm.reshape(HD, 1), k_norm.reshape(HD, 1))
    score_bound = (LOG2_E * HD ** 0.5 * BF16_ROUNDING_SLACK) * jnp.max(jnp.abs(q_norm)) * jnp.max(jnp.abs(k_norm))
    oa = _global_attention(qa_t, ka, va_t, score_bound)
    ob = _nbr_attention(qb_t, kb, vb_t, _nbr_tables(rpb, s // GRID_W))
    km_t, vm = _memory_kv(mem, w_mem_kv.astype(BF16))
    return _epilogue(x, oa, ob, sa, sb, sm, qm, sg, km_t, vm, w_branch.astype(BF16), w_out.astype(BF16),
                     ln_g.reshape(1, D_MODEL), ln_b.reshape(1, D_MODEL), alpha)


def kernel(x_prompt, x_sample, mem_prompt, mem_sample, w_in, q_norm, k_norm, rpb, w_mem_kv, w_branch, w_out,
           ln_g, ln_b):
    depth = w_in.shape[0]
    alpha = (2 * depth) ** 0.25
    y_prompt, y_sample = x_prompt, x_sample
    for l in range(depth):
        args = (w_in[l], q_norm[l], k_norm[l], rpb[l], w_mem_kv[l], w_branch[l], w_out[l], ln_g[l], ln_b[l])
        y_prompt = _layer(y_prompt, mem_prompt, *args, alpha)
        y_sample = _layer(y_sample, mem_sample, *args, alpha)
    return (y_prompt, y_sample)
```
